```python
import jax, jax.numpy as jnp
from jax import lax
import numpy as np


D_MODEL = 1024
BATCH = 2
SEQ = 16384
DEPTH = 4

N_A_LAYERS = DEPTH // 2
N_B_LAYERS = DEPTH - N_A_LAYERS
HGRN_EXPAND = 128
HGRN_HEADS = D_MODEL // HGRN_EXPAND
HGRN_DK = HGRN_EXPAND
HGRN_DV = D_MODEL // HGRN_HEADS
HGRN_D_KEY = HGRN_HEADS * HGRN_DK
HGRN_D_VAL = HGRN_HEADS * HGRN_DV
HGRN_CHUNK = 64
NSA_HEADS = 16
NSA_KV_HEADS = 4
NSA_GROUP = NSA_HEADS // NSA_KV_HEADS
NSA_HEAD_DIM = D_MODEL // NSA_HEADS
NSA_Q_WIDTH = NSA_HEADS * NSA_HEAD_DIM
N_BRANCH = 3
CMP_BLOCK = 32
CMP_STRIDE = 16
CMP_HIDDEN = 256
SEL_BLOCK = 64
N_SELECT = 16
WINDOW = 512
Q_BLOCK = 128
ROPE_THETA = 10000.0
FFN_HIDDEN = -(-8 * D_MODEL // (3 * 256)) * 256
EPS = 1e-6
NEG_INF = -1e30
FORCE_SCORE = 1e9

kernel_name = 'yoco_hgrn2_nsa_hybrid'


def rms_norm(x, gain):
    x32 = x.astype(jnp.float32)
    y = x32 * lax.rsqrt(jnp.mean(x32 * x32, axis=-1, keepdims=True) + EPS)
    return (y * gain.astype(jnp.float32)).astype(x.dtype)


def modulate(x, gain, shift, scale):
    return rms_norm(x, gain) * (1 + scale[:, None, :]) + shift[:, None, :]


def rope_tables(seq, dim):
    inv_freq = 1.0 / (ROPE_THETA ** (jnp.arange(0, dim, 2, dtype=jnp.float32) / dim))
    ang = jnp.arange(seq, dtype=jnp.float32)[:, None] * inv_freq[None, :]
    return jnp.cos(ang), jnp.sin(ang)


def apply_rope(x, cos, sin):
    x1, x2 = jnp.split(x, 2, axis=-1)
    cs = cos[None, :, None, :].astype(x.dtype)
    sn = sin[None, :, None, :].astype(x.dtype)
    return jnp.concatenate([x1 * cs - x2 * sn, x2 * cs + x1 * sn], axis=-1)


def masked_softmax(s, mask):
    p = jax.nn.softmax(jnp.where(mask, s.astype(jnp.float32), NEG_INF), axis=-1)
    return jnp.where(mask, p, 0.0)


def swiglu(h, w_in, w_out):
    a, b = jnp.split(h @ w_in, 2, axis=-1)
    return (jax.nn.silu(a) * b) @ w_out


def hgrn2_mixer(h, w_in, lower_bound, out_norm, w_out, first_layer):
    B, S, _ = h.shape
    q, f, i, g = jnp.split(h @ w_in, [HGRN_D_KEY, 2 * HGRN_D_KEY, 2 * HGRN_D_KEY + HGRN_D_VAL], axis=-1)
    q = jax.nn.silu(q)
    f32 = f.astype(jnp.float32)
    if first_layer:
        log_f = jax.nn.log_sigmoid(f32)
    else:
        lb = lower_bound.astype(jnp.float32)
        log_f = jnp.log(lb + (1.0 - lb) * jax.nn.sigmoid(f32))
    k = -jnp.expm1(log_f)
    nc = S // HGRN_CHUNK

    def to_chunks(t, d):
        return t.astype(jnp.float32).reshape(B, nc, HGRN_CHUNK, HGRN_HEADS, d).transpose(1, 0, 3, 2, 4)

    qc, kc, gc = to_chunks(q, HGRN_DK), to_chunks(k, HGRN_DK), to_chunks(log_f, HGRN_DK)
    vc = to_chunks(i, HGRN_DV)
    causal = jnp.tril(jnp.ones((HGRN_CHUNK, HGRN_CHUNK), dtype=bool))[:, :, None]

    def step(state, inp):
        q_, k_, v_, g_ = inp
        G = jnp.cumsum(g_, axis=2)
        o_inter = jnp.einsum('bhtk,bhkv->bhtv', q_ * jnp.exp(G), state)
        diff = G[:, :, :, None, :] - G[:, :, None, :, :]
        decay = jnp.where(causal, jnp.exp(jnp.where(causal, diff, 0.0)), 0.0)
        attn = jnp.einsum('bhtk,bhsk,bhtsk->bhts', q_, k_, decay)
        o_intra = jnp.einsum('bhts,bhsv->bhtv', attn, v_)
        G_last = G[:, :, -1:, :]
        new_state = jnp.exp(G_last[:, :, 0, :])[..., None] * state + jnp.einsum(
            'bhsk,bhsv->bhkv', k_ * jnp.exp(G_last - G), v_)
        return new_state, o_inter + o_intra

    s0 = jnp.zeros((B, HGRN_HEADS, HGRN_DK, HGRN_DV), jnp.float32)
    _, o = lax.scan(step, s0, (qc, kc, vc, gc))
    o = o.transpose(1, 0, 3, 2, 4).reshape(B, S, HGRN_HEADS, HGRN_DV)
    o = rms_norm(o, out_norm) * jax.nn.silu(g.reshape(B, S, HGRN_HEADS, HGRN_DV).astype(jnp.float32))
    return o.reshape(B, S, HGRN_D_VAL).astype(h.dtype) @ w_out


def nsa_shared_kv(h, w_kv, k_norm, cmp_pos, cmp_w1, cmp_w2, cos, sin):
    B, S, _ = h.shape
    kv = (h @ w_kv).reshape(B, S, 2 * N_BRANCH, NSA_KV_HEADS, NSA_HEAD_DIM)
    k_cmp, v_cmp, k_sel, v_sel, k_win, v_win = [kv[:, :, j] for j in range(2 * N_BRANCH)]
    n_sub = CMP_BLOCK // CMP_STRIDE
    n_cmp = S // CMP_STRIDE - n_sub + 1

    def compress(t, pos, w1, w2):
        r = t.reshape(B, S // CMP_STRIDE, CMP_STRIDE, NSA_KV_HEADS, NSA_HEAD_DIM)
        blocks = jnp.concatenate([r[:, j:j + n_cmp] for j in range(n_sub)], axis=2)
        blocks = blocks + pos[None, None, :, None, :]
        flat = blocks.transpose(0, 1, 3, 2, 4).reshape(B, n_cmp, NSA_KV_HEADS, CMP_BLOCK * NSA_HEAD_DIM)
        return jax.nn.silu(flat @ w1) @ w2

    kc = rms_norm(compress(k_cmp, cmp_pos[0], cmp_w1[0], cmp_w2[0]), k_norm[0])
    vc = compress(v_cmp, cmp_pos[1], cmp_w1[1], cmp_w2[1])
    n_sel = S // SEL_BLOCK
    ks = apply_rope(rms_norm(k_sel, k_norm[1]), cos, sin)
    ks_blocks = ks.reshape(B, n_sel, SEL_BLOCK, NSA_KV_HEADS, NSA_HEAD_DIM).transpose(0, 3, 1, 2, 4)
    vs_blocks = v_sel.reshape(B, n_sel, SEL_BLOCK, NSA_KV_HEADS, NSA_HEAD_DIM).transpose(0, 3, 1, 2, 4)
    kw = apply_rope(rms_norm(k_win, k_norm[2]), cos, sin)
    pad = ((0, 0), (WINDOW, 0), (0, 0), (0, 0))
    return (kc, vc, ks_blocks, vs_blocks, jnp.pad(kw, pad), jnp.pad(v_win, pad))


def nsa_mixer(h, w_q, q_norm, w_out, shared, cos, sin):
    B, S, _ = h.shape
    kc, vc, ks_blocks, vs_blocks, kw_pad, vw_pad = shared
    proj = h @ w_q
    q = rms_norm(proj[..., :NSA_Q_WIDTH].reshape(B, S, NSA_HEADS, NSA_HEAD_DIM), q_norm)
    gates = jax.nn.sigmoid(proj[..., NSA_Q_WIDTH:].astype(jnp.float32)).reshape(B, S, N_BRANCH, NSA_HEADS)
    q_rope = apply_rope(q, cos, sin)
    n_cmp = kc.shape[1]
    n_sel = ks_blocks.shape[2]
    n_top = min(N_SELECT, n_sel)
    cmp_end = jnp.arange(n_cmp) * CMP_STRIDE + CMP_BLOCK - 1
    sel_start = jnp.arange(n_sel) * SEL_BLOCK
    cs = np.arange(n_cmp) * CMP_STRIDE
    ss = np.arange(n_sel) * SEL_BLOCK
    overlap = jnp.asarray(((cs[:, None] < ss[None, :] + SEL_BLOCK) &
                           (cs[:, None] + CMP_BLOCK > ss[None, :])).astype(np.float32))
    scale = NSA_HEAD_DIM ** -0.5
    nb = S // Q_BLOCK
    bi = jnp.arange(B)[:, None, None, None]
    hi = jnp.arange(NSA_KV_HEADS)[None, :, None, None]
    blk_pos = jnp.arange(SEL_BLOCK)

    def blockify(t):
        return t.reshape(B, nb, Q_BLOCK, *t.shape[2:]).swapaxes(0, 1)

    def attend(args):
        blk, qp, qr, g = args
        t = blk * Q_BLOCK + jnp.arange(Q_BLOCK)
        qp = qp.reshape(B, Q_BLOCK, NSA_KV_HEADS, NSA_GROUP, NSA_HEAD_DIM)
        qr = qr.reshape(B, Q_BLOCK, NSA_KV_HEADS, NSA_GROUP, NSA_HEAD_DIM)
        s_c = jnp.einsum('bqhgd,bchd->bhgqc', qp, kc) * scale
        p_c = masked_softmax(s_c, cmp_end[None, :] <= t[:, None])
        o_c = jnp.einsum('bhgqc,bchd->bqhgd', p_c.astype(vc.dtype), vc)
        imp = jnp.einsum('bhgqc,cj->bhqj', p_c, overlap)
        cur = t // SEL_BLOCK
        j = jnp.arange(n_sel)[None, :]
        forced = (j == 0) | (j == cur[:, None]) | (j == cur[:, None] - 1)
        valid_blk = sel_start[None, :] <= t[:, None]
        imp = jnp.where(forced & valid_blk, FORCE_SCORE, jnp.where(valid_blk, imp, -1.0))
        top_val, top_idx = lax.top_k(imp, n_top)
        ks_g = ks_blocks[bi, hi, top_idx].reshape(B, NSA_KV_HEADS, Q_BLOCK, n_top * SEL_BLOCK, NSA_HEAD_DIM)
        vs_g = vs_blocks[bi, hi, top_idx].reshape(B, NSA_KV_HEADS, Q_BLOCK, n_top * SEL_BLOCK, NSA_HEAD_DIM)
        kpos = (top_idx[..., None] * SEL_BLOCK + blk_pos).reshape(B, NSA_KV_HEADS, Q_BLOCK, n_top * SEL_BLOCK)
        m_s = jnp.repeat(top_val >= 0, SEL_BLOCK, axis=-1) & (kpos <= t[None, None, :, None])
        s_s = jnp.einsum('bqhgd,bhqkd->bhgqk', qr, ks_g) * scale
        p_s = masked_softmax(s_s, m_s[:, :, None])
        o_s = jnp.einsum('bhgqk,bhqkd->bqhgd', p_s.astype(vs_g.dtype), vs_g)
        kw = lax.dynamic_slice_in_dim(kw_pad, blk * Q_BLOCK, Q_BLOCK + WINDOW, axis=1)
        vw = lax.dynamic_slice_in_dim(vw_pad, blk * Q_BLOCK, Q_BLOCK + WINDOW, axis=1)
        kpos_w = blk * Q_BLOCK - WINDOW + jnp.arange(Q_BLOCK + WINDOW)
        m_w = ((kpos_w[None, :] <= t[:, None]) & (kpos_w[None, :] > t[:, None] - WINDOW)
               & (kpos_w[None, :] >= 0))
        s_w = jnp.einsum('bqhgd,bkhd->bhgqk', qr, kw) * scale
        p_w = masked_softmax(s_w, m_w)
        o_w = jnp.einsum('bhgqk,bkhd->bqhgd', p_w.astype(vw.dtype), vw)
        g = g.reshape(B, Q_BLOCK, N_BRANCH, NSA_KV_HEADS, NSA_GROUP)[..., None]
        o = g[:, :, 0] * o_c + g[:, :, 1] * o_s + g[:, :, 2] * o_w
        return o.reshape(B, Q_BLOCK, NSA_Q_WIDTH).astype(h.dtype)

    out = lax.map(attend, (jnp.arange(nb), blockify(q), blockify(q_rope), blockify(gates)))
    return out.swapaxes(0, 1).reshape(B, S, NSA_Q_WIDTH) @ w_out


def setup_inputs(seed: int = 0) -> dict:
    key = jax.random.key(seed)
    ks = jax.random.split(key, 24)
    D = D_MODEL

    def nrm(k, shape, scale):
        return jax.random.normal(k, shape, jnp.float32) * scale

    def gain(k, shape):
        return 1.0 + nrm(k, shape, 0.02)

    return {
        'x': nrm(ks[0], (BATCH, SEQ, D), 1.0),
        'c': nrm(ks[1], (BATCH, D), 1.0),
        'ada_w': nrm(ks[2], (2 * DEPTH, D, 3 * D), 0.5 * D ** -0.5),
        'ada_b': nrm(ks[3], (2 * DEPTH, 3 * D), 0.02),
        'norm_mix': gain(ks[4], (DEPTH, D)),
        'norm_ffn': gain(ks[5], (DEPTH, D)),
        'hgrn_w_in': nrm(ks[6], (N_A_LAYERS, D, 2 * HGRN_D_KEY + 2 * HGRN_D_VAL), D ** -0.5),
        'hgrn_lower_bounds': nrm(ks[7], (N_A_LAYERS, HGRN_D_KEY), 0.5),
        'hgrn_out_norm': gain(ks[8], (N_A_LAYERS, HGRN_DV)),
        'hgrn_w_out': nrm(ks[9], (N_A_LAYERS, HGRN_D_VAL, D), HGRN_D_VAL ** -0.5),
        'kv_ada_w': nrm(ks[10], (D, 2 * D), 0.5 * D ** -0.5),
        'kv_ada_b': nrm(ks[11], (2 * D,), 0.02),
        'kv_norm': gain(ks[12], (D,)),
        'nsa_w_kv': nrm(ks[13], (D, 2 * N_BRANCH * NSA_KV_HEADS * NSA_HEAD_DIM), D ** -0.5),
        'nsa_k_norm': gain(ks[14], (N_BRANCH, NSA_HEAD_DIM)),
        'cmp_pos': nrm(ks[15], (2, CMP_BLOCK, NSA_HEAD_DIM), 0.1),
        'cmp_w1': nrm(ks[16], (2, CMP_BLOCK * NSA_HEAD_DIM, CMP_HIDDEN), (CMP_BLOCK * NSA_HEAD_DIM) ** -0.5),
        'cmp_w2': nrm(ks[17], (2, CMP_HIDDEN, NSA_HEAD_DIM), CMP_HIDDEN ** -0.5),
        'nsa_w_q': nrm(ks[18], (N_B_LAYERS, D, NSA_Q_WIDTH + N_BRANCH * NSA_HEADS), D ** -0.5),
        'nsa_q_norm': gain(ks[19], (N_B_LAYERS, NSA_HEAD_DIM)),
        'nsa_w_out': nrm(ks[20], (N_B_LAYERS, NSA_Q_WIDTH, D), NSA_Q_WIDTH ** -0.5),
        'ffn_w_in': nrm(ks[21], (DEPTH, D, 2 * FFN_HIDDEN), D ** -0.5),
        'ffn_w_out': nrm(ks[22], (DEPTH, FFN_HIDDEN, D), FFN_HIDDEN ** -0.5),
    }


def reference(x, c, ada_w, ada_b, norm_mix, norm_ffn, hgrn_w_in, hgrn_lower_bounds, hgrn_out_norm,
              hgrn_w_out, kv_ada_w, kv_ada_b, kv_norm, nsa_w_kv, nsa_k_norm, cmp_pos, cmp_w1, cmp_w2,
              nsa_w_q, nsa_q_norm, nsa_w_out, ffn_w_in, ffn_w_out):
    B, S, _ = x.shape
    cos, sin = rope_tables(S, NSA_HEAD_DIM)
    c_act = jax.nn.silu(c)
    mods = jnp.einsum('bd,lde->lbe', c_act, ada_w) + ada_b[:, None, :]
    lb = jax.nn.softmax(hgrn_lower_bounds.astype(jnp.float32), axis=0)
    lb = jnp.cumsum(lb, axis=0) - lb[0]
    shared = None
    for layer in range(DEPTH):
        shift, scale, gate = jnp.split(mods[2 * layer], 3, axis=-1)
        h = modulate(x, norm_mix[layer], shift, scale)
        if layer < N_A_LAYERS:
            y = hgrn2_mixer(h, hgrn_w_in[layer], lb[layer], hgrn_out_norm[layer], hgrn_w_out[layer], layer == 0)
        else:
            if layer == N_A_LAYERS:
                kv_shift, kv_scale = jnp.split(c_act @ kv_ada_w + kv_ada_b, 2, axis=-1)
                shared = nsa_shared_kv(modulate(x, kv_norm, kv_shift, kv_scale), nsa_w_kv, nsa_k_norm,
                                       cmp_pos, cmp_w1, cmp_w2, cos, sin)
            bl = layer - N_A_LAYERS
            y = nsa_mixer(h, nsa_w_q[bl], nsa_q_norm[bl], nsa_w_out[bl], shared, cos, sin)
        x = x + gate[:, None, :] * y
        shift, scale, gate = jnp.split(mods[2 * layer + 1], 3, axis=-1)
        h = modulate(x, norm_ffn[layer], shift, scale)
        x = x + gate[:, None, :] * swiglu(h, ffn_w_in[layer], ffn_w_out[layer])
    return x
```

```python
import functools

import jax
import jax.numpy as jnp
from jax import lax
from jax.experimental import pallas as pl
from jax.experimental.pallas import tpu as pltpu

F32 = jnp.float32
BF16 = jnp.bfloat16

D_MODEL = 1024
DEPTH = 4
N_A_LAYERS = DEPTH // 2
HGRN_HEADS = 8
HGRN_DK = 128
NSA_HEADS = 16
NSA_KV_HEADS = 4
NSA_GROUP = 4
HEAD_DIM = 64
N_BRANCH = 3
CMP_BLOCK = 32
CMP_STRIDE = 16
CMP_HIDDEN = 256
SEL_BLOCK = 64
N_SELECT = 16
WINDOW = 512
ROPE_THETA = 10000.0
FFN_HIDDEN = 2816
EPS = 1e-6
NEG_INF = -1e30
FORCE_SCORE = 1e9

LANES = 128
MXU_N = 256
VMEM_LIMIT = 56 * 1024 * 1024

ROW_TILE = 512
HGRN_CHUNK = 128
Q_TILE = 128
SEL_KEYS = 256
FFN_CHUNK = 256


def _params(sem):
    return pltpu.CompilerParams(dimension_semantics=sem, vmem_limit_bytes=VMEM_LIMIT)


def _const_spec(shape):
    nd = len(shape)
    return pl.BlockSpec(shape, lambda *_: (0,) * nd, pipeline_mode=pl.Buffered(1))


def _dot(a, b):
    return jnp.dot(a, b, preferred_element_type=F32)


def _dot_nt(a, b):
    return lax.dot_general(a, b, (((1,), (1,)), ((), ())), preferred_element_type=F32)


def _dot_tn(a, b):
    return lax.dot_general(a, b, (((0,), (0,)), ((), ())), preferred_element_type=F32)


def _split(a):
    hi = a.astype(BF16)
    lo = (a - hi.astype(F32)).astype(BF16)
    return hi, lo


def _sigmoid(x):
    return 1.0 / (1.0 + jnp.exp(-x))


def _silu(x):
    return x * _sigmoid(x)


def _mod_norm(x, gain, shift, scale):
    ms = jnp.mean(x * x, axis=-1, keepdims=True)
    y = x * lax.rsqrt(ms + EPS) * gain
    return y * (1.0 + scale) + shift


def _mods_kernel(c_ref, w_ref, b_ref, o_ref):
    c = c_ref[...]
    ah, al = _split(_silu(c))
    wh, wl = _split(w_ref[0])
    o_ref[0] = _dot(ah, wh) + _dot(al, wh) + _dot(ah, wl) + b_ref[0]


def _mods(c_pad, w, b):
    n_l, _, n = w.shape
    tn = 1024
    return pl.pallas_call(
        _mods_kernel,
        grid=(n_l, n // tn),
        in_specs=[
            pl.BlockSpec((8, D_MODEL), lambda l, j: (0, 0)),
            pl.BlockSpec((1, D_MODEL, tn), lambda l, j: (l, 0, j)),
            pl.BlockSpec((1, 1, tn), lambda l, j: (l, 0, j)),
        ],
        out_specs=pl.BlockSpec((1, 8, tn), lambda l, j: (l, 0, j)),
        out_shape=jax.ShapeDtypeStruct((n_l, 8, n), F32),
        compiler_params=_params(("arbitrary", "arbitrary")),
        name="adaln_mods",
    )(c_pad, w, b.reshape(n_l, 1, n))


def _ffn_kernel(x_ref, mod_ref, gain_ref, win_ref, wout_ref, o_ref):
    x = x_ref[...]
    m = mod_ref[0]
    h = _mod_norm(x, gain_ref[...], m[0:1], m[1:2]).astype(BF16)
    acc = jnp.zeros(x.shape, F32)
    for c in range(FFN_HIDDEN // FFN_CHUNK):
        lo = c * FFN_CHUNK
        a = _dot(h, win_ref[:, lo:lo + FFN_CHUNK])
        b = _dot(h, win_ref[:, FFN_HIDDEN + lo:FFN_HIDDEN + lo + FFN_CHUNK])
        g = (_silu(a) * b).astype(BF16)
        acc = acc + _dot(g, wout_ref[lo:lo + FFN_CHUNK, :])
    o_ref[...] = x + m[2:3] * acc


def _ffn(x2, mod, gain, w_in, w_out, seq):
    t = x2.shape[0]
    per_b = seq // ROW_TILE
    return pl.pallas_call(
        _ffn_kernel,
        grid=(t // ROW_TILE,),
        in_specs=[
            pl.BlockSpec((ROW_TILE, D_MODEL), lambda i: (i, 0)),
            pl.BlockSpec((1, 3, D_MODEL), lambda i: (i // per_b, 0, 0)),
            _const_spec((1, D_MODEL)),
            _const_spec((D_MODEL, 2 * FFN_HIDDEN)),
            _const_spec((FFN_HIDDEN, D_MODEL)),
        ],
        out_specs=pl.BlockSpec((ROW_TILE, D_MODEL), lambda i: (i, 0)),
        out_shape=jax.ShapeDtypeStruct(x2.shape, F32),
        compiler_params=_params(("arbitrary",)),
        name="ffn",
    )(x2, mod, gain, w_in, w_out)


def _hgrn_kernel(x_ref, mod_ref, gain_ref, win_ref, lbraw_ref, onorm_ref, wout_ref,
                 o_ref, proj_scr, st_scr, oall_scr, *, layer):
    tc = HGRN_CHUNK
    dk = HGRN_DK

    @pl.when(pl.program_id(1) == 0)
    def _():
        st_scr[...] = jnp.zeros(st_scr.shape, F32)

    x = x_ref[...]
    m = mod_ref[0]
    h = _mod_norm(x, gain_ref[...], m[0:1], m[1:2]).astype(BF16)
    proj_scr[...] = _dot(h, win_ref[...])

    if layer > 0:
        raw = lbraw_ref[...]
        e = jnp.exp(raw - jnp.max(raw, axis=0, keepdims=True))
        sm = e / jnp.sum(e, axis=0, keepdims=True)
        lb = jnp.sum(sm[1:layer + 1], axis=0, keepdims=True)

    row = lax.broadcasted_iota(jnp.int32, (tc, D_MODEL), 0)
    r_i = lax.broadcasted_iota(jnp.int32, (tc, tc), 0)
    c_i = lax.broadcasted_iota(jnp.int32, (tc, tc), 1)
    ones_b = jnp.ones((dk, dk), BF16)
    n_lvl = tc.bit_length() - 1

    def chunk(ci, carry):
        r0 = pl.multiple_of(ci * tc, tc)
        qp = proj_scr[pl.ds(r0, tc), 0:D_MODEL]
        fp = proj_scr[pl.ds(r0, tc), D_MODEL:2 * D_MODEL]
        v = proj_scr[pl.ds(r0, tc), 2 * D_MODEL:3 * D_MODEL].astype(BF16)
        gp = proj_scr[pl.ds(r0, tc), 3 * D_MODEL:4 * D_MODEL]

        q = _silu(qp)
        e = jnp.exp(-jnp.abs(fp))
        r = 1.0 / (1.0 + e)
        pos = fp >= 0.0
        sig = jnp.where(pos, r, e * r)
        nsig = jnp.where(pos, e * r, r)
        if layer == 0:
            logf = jnp.minimum(fp, 0.0) - jnp.log(1.0 + e)
            kk = nsig
        else:
            logf = jnp.log(lb + (1.0 - lb) * sig)
            kk = (1.0 - lb) * nsig

        g_cum = logf
        s = 1
        while s < tc:
            g_cum = g_cum + jnp.where(row >= s, pltpu.roll(g_cum, s, 0), 0.0)
            s *= 2

        p_acc = [jnp.zeros((tc, tc), F32) for _ in range(HGRN_HEADS)]
        end_val = g_cum
        for lvl in range(n_lvl):
            hs = 1 << lvl
            second = (row & hs) != 0
            ref_val = jnp.where(second, pltpu.roll(end_val, hs, 0), end_val)
            decay = jnp.exp(-jnp.abs(g_cum - ref_val))
            qs = jnp.where(second, q * decay, 0.0).astype(BF16)
            ks = jnp.where(second, 0.0, kk * decay).astype(BF16)
            same = (r_i >> (lvl + 1)) == (c_i >> (lvl + 1))
            for hd in range(HGRN_HEADS):
                sl = slice(hd * dk, (hd + 1) * dk)
                p_acc[hd] = p_acc[hd] + jnp.where(same, _dot_nt(qs[:, sl], ks[:, sl]), 0.0)
            if lvl + 1 < n_lvl:
                end_val = jnp.where(second, end_val, pltpu.roll(end_val, tc - hs, 0))

        qk = (q * kk).astype(BF16)
        g_last = g_cum[tc - 1:tc, :]
        qe = (q * jnp.exp(g_cum)).astype(BF16)
        kd = (kk * jnp.exp(g_last - g_cum)).astype(BF16)
        s_decay = jnp.exp(g_last)
        gate_act = _silu(gp)
        onorm = onorm_ref[...]
        for hd in range(HGRN_HEADS):
            sl = slice(hd * dk, (hd + 1) * dk)
            p_h = p_acc[hd] + jnp.where(r_i == c_i, _dot(qk[:, sl], ones_b), 0.0)
            st = st_scr[hd]
            o_h = _dot(p_h.astype(BF16), v[:, sl]) + _dot_nt(qe[:, sl], st.astype(BF16))
            st_scr[hd] = s_decay[:, sl] * st + _dot_tn(v[:, sl], kd[:, sl])
            ms = jnp.mean(o_h * o_h, axis=-1, keepdims=True)
            o_n = o_h * lax.rsqrt(ms + EPS) * onorm
            oall_scr[pl.ds(r0, tc), sl] = (o_n * gate_act[:, sl]).astype(BF16)
        return carry

    lax.fori_loop(0, ROW_TILE // tc, chunk, 0)
    o_ref[...] = x + m[2:3] * _dot(oall_scr[...], wout_ref[...])


def _hgrn(x2, mod, gain, w_in, lb_raw, onorm, w_out, batch, seq, layer):
    per_b = seq // ROW_TILE
    return pl.pallas_call(
        functools.partial(_hgrn_kernel, layer=layer),
        grid=(batch, per_b),
        in_specs=[
            pl.BlockSpec((ROW_TILE, D_MODEL), lambda b, j: (b * per_b + j, 0)),
            pl.BlockSpec((1, 3, D_MODEL), lambda b, j: (b, 0, 0)),
            _const_spec((1, D_MODEL)),
            _const_spec((D_MODEL, 4 * D_MODEL)),
            _const_spec((N_A_LAYERS, D_MODEL)),
            _const_spec((1, HGRN_DK)),
            _const_spec((D_MODEL, D_MODEL)),
        ],
        out_specs=pl.BlockSpec((ROW_TILE, D_MODEL), lambda b, j: (b * per_b + j, 0)),
        out_shape=jax.ShapeDtypeStruct(x2.shape, F32),
        scratch_shapes=[
            pltpu.VMEM((ROW_TILE, 4 * D_MODEL), F32),
            pltpu.VMEM((HGRN_HEADS, HGRN_DK, HGRN_DK), F32),
            pltpu.VMEM((ROW_TILE, D_MODEL), BF16),
        ],
        compiler_params=_params(("arbitrary", "arbitrary")),
        name=f"hgrn{layer}",
    )(x2, mod, gain, w_in, lb_raw, onorm, w_out)


def _segnorm64(xc, seg_ones, gain):
    hi, lo = _split(xc * xc)
    ss = _dot(hi, seg_ones) + _dot(lo, seg_ones)
    return xc * lax.rsqrt(ss * (1.0 / HEAD_DIM) + EPS) * gain


def _rope_lanes(xp, cosn, sinn):
    lane = lax.broadcasted_iota(jnp.int32, xp.shape, 1)
    first = (lane & (HEAD_DIM // 2)) == 0
    rot = jnp.where(first, pltpu.roll(xp, LANES - HEAD_DIM // 2, 1), pltpu.roll(xp, HEAD_DIM // 2, 1))
    return xp * cosn + rot * sinn


def _rope_wide(xc, cosn, sinn):
    return jnp.concatenate(
        [_rope_lanes(xc[:, i * LANES:(i + 1) * LANES], cosn, sinn) for i in range(xc.shape[1] // LANES)], axis=1)


def _kv_kernel(x_ref, mod_ref, gain_ref, wn_ref, wvt_ref, seg_ref, gsel_ref, gwin_ref, cos_ref, sin_ref,
               kcmp_ref, vcmp_ref, ksel_ref, kwin_ref, vselt_ref, vwint_ref):
    x = x_ref[...]
    m = mod_ref[0]
    h = _mod_norm(x, gain_ref[...], m[0:1], m[1:2]).astype(BF16)
    nat = _dot(h, wn_ref[...])
    vt = _dot_nt(wvt_ref[...], h)
    w = NSA_KV_HEADS * HEAD_DIM
    seg = seg_ref[...]
    cosn = cos_ref[...]
    sinn = sin_ref[...]
    kcmp = nat[:, 0:w].astype(BF16)
    vcmp = nat[:, w:2 * w].astype(BF16)
    ksel = _rope_wide(_segnorm64(nat[:, 2 * w:3 * w], seg, gsel_ref[...]), cosn, sinn).astype(BF16)
    kwin = _rope_wide(_segnorm64(nat[:, 3 * w:4 * w], seg, gwin_ref[...]), cosn, sinn).astype(BF16)
    for hd in range(NSA_KV_HEADS):
        sl = slice(hd * HEAD_DIM, (hd + 1) * HEAD_DIM)
        kcmp_ref[0, hd] = kcmp[:, sl]
        vcmp_ref[0, hd] = vcmp[:, sl]
        ksel_ref[0, hd] = ksel[:, sl]
        kwin_ref[0, hd] = kwin[:, sl]
        vselt_ref[0, hd] = vt[hd * HEAD_DIM:(hd + 1) * HEAD_DIM, :].astype(BF16)
        vwint_ref[0, hd] = vt[w + hd * HEAD_DIM:w + (hd + 1) * HEAD_DIM, :].astype(BF16)


def _kv_prep(x2, mod, gain, w_nat, w_vt, seg, gsel, gwin, cosn, sinn, batch, seq):
    per_b = seq // ROW_TILE
    w = NSA_KV_HEADS * HEAD_DIM
    nat_spec = pl.BlockSpec((1, NSA_KV_HEADS, ROW_TILE, HEAD_DIM), lambda b, j: (b, 0, j, 0))
    tr_spec = pl.BlockSpec((1, NSA_KV_HEADS, HEAD_DIM, ROW_TILE), lambda b, j: (b, 0, 0, j))
    nat_shape = jax.ShapeDtypeStruct((batch, NSA_KV_HEADS, seq, HEAD_DIM), BF16)
    tr_shape = jax.ShapeDtypeStruct((batch, NSA_KV_HEADS, HEAD_DIM, seq), BF16)
    return pl.pallas_call(
        _kv_kernel,
        grid=(batch, per_b),
        in_specs=[
            pl.BlockSpec((ROW_TILE, D_MODEL), lambda b, j: (b * per_b + j, 0)),
            pl.BlockSpec((1, 2, D_MODEL), lambda b, j: (b, 0, 0)),
            _const_spec((1, D_MODEL)),
            _const_spec((D_MODEL, 4 * w)),
            _const_spec((2 * w, D_MODEL)),
            _const_spec((w, w)),
            _const_spec((1, w)),
            _const_spec((1, w)),
            pl.BlockSpec((ROW_TILE, LANES), lambda b, j: (j, 0)),
            pl.BlockSpec((ROW_TILE, LANES), lambda b, j: (j, 0)),
        ],
        out_specs=[nat_spec, nat_spec, nat_spec, nat_spec, tr_spec, tr_spec],
        out_shape=[nat_shape, nat_shape, nat_shape, nat_shape, tr_shape, tr_shape],
        compiler_params=_params(("arbitrary", "arbitrary")),
        name="nsa_kv",
    )(x2, mod, gain, w_nat, w_vt, seg, gsel, gwin, cosn, sinn)


def _compress_pre(r_ref, w1_ref, pos_ref):
    r = r_ref[0, 0]
    half = CMP_STRIDE * HEAD_DIM
    w1a = w1_ref[0:half, :]
    w1b = w1_ref[half:2 * half, :]
    n_rows = r.shape[0]
    u = _dot(r, w1a)
    v = _dot(r, w1b)
    pos = pos_ref[...]
    ph, pl_ = _split(pos)
    bias = (_dot(ph[:, 0:half], w1a) + _dot(pl_[:, 0:half], w1a)
            + _dot(ph[:, half:], w1b) + _dot(pl_[:, half:], w1b))[0:1, :]
    pre = u + pltpu.roll(v, n_rows - 1, 0) + bias
    return _silu(pre).astype(BF16)


def _compress_k_kernel(r_ref, w1_ref, pos_ref, w2_ref, gain_ref, o_ref):
    hid = _compress_pre(r_ref, w1_ref, pos_ref)
    out = _dot(hid, w2_ref[...])
    ms = jnp.mean(out * out, axis=-1, keepdims=True)
    o_ref[0, 0] = (out * lax.rsqrt(ms + EPS) * gain_ref[...]).astype(BF16)


def _compress_v_kernel(r_ref, w1_ref, pos_ref, w2t_ref, o_ref):
    hid = _compress_pre(r_ref, w1_ref, pos_ref)
    o_ref[0, 0] = _dot_nt(w2t_ref[...], hid).astype(BF16)


def _compress(r, w1, pos8, w2, gain, batch, n_rows, transposed):
    half2 = CMP_BLOCK * HEAD_DIM
    in_specs = [
        pl.BlockSpec((1, 1, n_rows, CMP_STRIDE * HEAD_DIM), lambda b, hd: (b, hd, 0, 0)),
        _const_spec((half2, CMP_HIDDEN)),
        _const_spec((8, half2)),
    ]
    if transposed:
        kern = _compress_v_kernel
        in_specs.append(_const_spec((HEAD_DIM, CMP_HIDDEN)))
        args = (r, w1, pos8, w2)
        out_spec = pl.BlockSpec((1, 1, HEAD_DIM, n_rows), lambda b, hd: (b, hd, 0, 0))
        out_shape = jax.ShapeDtypeStruct((batch, NSA_KV_HEADS, HEAD_DIM, n_rows), BF16)
    else:
        kern = _compress_k_kernel
        in_specs += [_const_spec((CMP_HIDDEN, HEAD_DIM)), _const_spec((1, HEAD_DIM))]
        args = (r, w1, pos8, w2, gain)
        out_spec = pl.BlockSpec((1, 1, n_rows, HEAD_DIM), lambda b, hd: (b, hd, 0, 0))
        out_shape = jax.ShapeDtypeStruct((batch, NSA_KV_HEADS, n_rows, HEAD_DIM), BF16)
    return pl.pallas_call(
        kern,
        grid=(batch, NSA_KV_HEADS),
        in_specs=in_specs,
        out_specs=out_spec,
        out_shape=out_shape,
        compiler_params=_params(("arbitrary", "arbitrary")),
        name="nsa_compress_v" if transposed else "nsa_compress_k",
    )(*args)


def _q_kernel(x_ref, mod_ref, gain_ref, wq_ref, wgt_ref, seg_ref, qgain_ref, cos_ref, sin_ref,
              qn_ref, qr_ref, gt_ref):
    x = x_ref[...]
    m = mod_ref[0]
    h = _mod_norm(x, gain_ref[...], m[0:1], m[1:2]).astype(BF16)
    gt_ref[0] = _sigmoid(_dot_nt(wgt_ref[...], h))
    seg = seg_ref[...]
    cosn = cos_ref[...]
    sinn = sin_ref[...]
    scale = HEAD_DIM ** -0.5
    w = NSA_GROUP * HEAD_DIM
    for c in range(NSA_KV_HEADS):
        qc = _dot(h, wq_ref[:, c * w:(c + 1) * w])
        qn = _segnorm64(qc, seg, qgain_ref[...]) * scale
        qr = _rope_wide(qn, cosn, sinn)
        qn = qn.astype(BF16)
        qr = qr.astype(BF16)
        for g in range(NSA_GROUP):
            sl = slice(g * HEAD_DIM, (g + 1) * HEAD_DIM)
            qn_ref[0, c * NSA_GROUP + g] = qn[:, sl]
            qr_ref[0, c * NSA_GROUP + g] = qr[:, sl]


def _q_proj(x2, mod, gain, w_q, w_gt, seg, qgain, cosn, sinn, batch, seq):
    per_b = seq // ROW_TILE
    w = NSA_GROUP * HEAD_DIM
    n_gate = N_BRANCH * NSA_HEADS
    q_spec = pl.BlockSpec((1, NSA_HEADS, ROW_TILE, HEAD_DIM), lambda b, j: (b, 0, j, 0))
    q_shape = jax.ShapeDtypeStruct((batch, NSA_HEADS, seq, HEAD_DIM), BF16)
    return pl.pallas_call(
        _q_kernel,
        grid=(batch, per_b),
        in_specs=[
            pl.BlockSpec((ROW_TILE, D_MODEL), lambda b, j: (b * per_b + j, 0)),
            pl.BlockSpec((1, 3, D_MODEL), lambda b, j: (b, 0, 0)),
            _const_spec((1, D_MODEL)),
            _const_spec((D_MODEL, D_MODEL)),
            _const_spec((n_gate, D_MODEL)),
            _const_spec((w, w)),
            _const_spec((1, w)),
            pl.BlockSpec((ROW_TILE, LANES), lambda b, j: (j, 0)),
            pl.BlockSpec((ROW_TILE, LANES), lambda b, j: (j, 0)),
        ],
        out_specs=[q_spec, q_spec, pl.BlockSpec((1, n_gate, ROW_TILE), lambda b, j: (b, 0, j))],
        out_shape=[q_shape, q_shape, jax.ShapeDtypeStruct((batch, n_gate, seq), F32)],
        compiler_params=_params(("arbitrary", "arbitrary")),
        name="nsa_q",
    )(x2, mod, gain, w_q, w_gt, seg, qgain, cosn, sinn)


def _attn_kernel(qn_ref, qr_ref, gt_ref, kc_ref, vct_ref, ks_ref, vst_ref, kw_ref, vwt_ref, ovl_ref,
                 o_ref, sel_scr, *, n_cmp, n_sel):
    qb = Q_TILE
    ng = NSA_GROUP
    hk = pl.program_id(1)
    qi = pl.program_id(2)
    t0 = qi * qb
    q_n = qn_ref[0].reshape(ng * qb, HEAD_DIM)
    q_r = qr_ref[0].reshape(ng * qb, HEAD_DIM)

    def tq(rows):
        lane = lax.broadcasted_iota(jnp.int32, (rows, ng * qb), 1)
        return t0 + (lane & (qb - 1))

    s_c = _dot_nt(kc_ref[0, 0], q_n)
    c_end = lax.broadcasted_iota(jnp.int32, (n_cmp, ng * qb), 0) * CMP_STRIDE + (CMP_BLOCK - 1)
    ok_c = c_end <= tq(n_cmp)
    s_c = jnp.where(ok_c, s_c, NEG_INF)
    e_c = jnp.where(ok_c, jnp.exp(s_c - jnp.max(s_c, axis=0, keepdims=True)), 0.0)
    l_c = jnp.sum(e_c, axis=0, keepdims=True)
    p_c = e_c * (1.0 / jnp.where(l_c > 0.0, l_c, 1.0))
    o_c = _dot(vct_ref[0, 0], p_c.astype(BF16))

    p_sum = p_c[:, 0:qb]
    for g in range(1, ng):
        p_sum = p_sum + p_c[:, g * qb:(g + 1) * qb]
    ph, pl_ = _split(p_sum)
    imp = _dot(ovl_ref[...], ph) + _dot(ovl_ref[...], pl_)
    j_i = lax.broadcasted_iota(jnp.int32, (n_sel, qb), 0)
    cur = (t0 + lax.broadcasted_iota(jnp.int32, (n_sel, qb), 1)) // SEL_BLOCK
    valid = j_i <= cur
    forced = (j_i == 0) | (j_i == cur) | (j_i == cur - 1)
    val = jnp.where(forced & valid, FORCE_SCORE, jnp.where(valid, imp, -1.0))
    sel = jnp.zeros((n_sel, qb), F32)
    for _ in range(min(N_SELECT, n_sel)):
        mx = jnp.max(val, axis=0, keepdims=True)
        idx = jnp.min(jnp.where(val == mx, j_i, n_sel), axis=0, keepdims=True)
        hit = j_i == idx
        sel = jnp.where(hit & (mx >= 0.0), 1.0, sel)
        val = jnp.where(hit, -3.0e38, val)
    sel_scr[...] = sel

    def online(s, ok, v_t, carry):
        m_old, l_old, acc = carry
        s = jnp.where(ok, s, NEG_INF)
        m_new = jnp.maximum(m_old, jnp.max(s, axis=0, keepdims=True))
        alpha = jnp.exp(m_old - m_new)
        p = jnp.where(ok, jnp.exp(s - m_new), 0.0)
        l_new = alpha * l_old + jnp.sum(p, axis=0, keepdims=True)
        return m_new, l_new, alpha * acc + _dot(v_t, p.astype(BF16))

    init = (jnp.full((1, ng * qb), NEG_INF, F32), jnp.zeros((1, ng * qb), F32),
            jnp.zeros((HEAD_DIM, ng * qb), F32))

    blocks_per_tile = SEL_KEYS // SEL_BLOCK

    def sel_step(kt, carry):
        k0 = pl.multiple_of(kt * SEL_KEYS, SEL_KEYS)
        s = _dot_nt(ks_ref[0, 0, pl.ds(k0, SEL_KEYS), :], q_r)
        rows = sel_scr[pl.ds(kt * blocks_per_tile, blocks_per_tile), :]
        chosen = jnp.concatenate(
            [jnp.broadcast_to(rows[i:i + 1, :], (SEL_BLOCK, qb)) for i in range(blocks_per_tile)], axis=0)
        chosen = jnp.concatenate([chosen] * ng, axis=1)
        kpos = k0 + lax.broadcasted_iota(jnp.int32, (SEL_KEYS, ng * qb), 0)
        ok = (chosen > 0.5) & (kpos <= tq(SEL_KEYS))
        return online(s, ok, vst_ref[0, 0, :, pl.ds(k0, SEL_KEYS)], carry)

    n_kt = (t0 + qb + SEL_KEYS - 1) // SEL_KEYS
    _, l_s, acc_s = lax.fori_loop(0, n_kt, sel_step, init)

    def win_step(wt, carry):
        k0 = pl.multiple_of(wt * qb, qb)
        s = _dot_nt(kw_ref[0, 0, pl.ds(k0, qb), :], q_r)
        kpos = k0 + lax.broadcasted_iota(jnp.int32, (qb, ng * qb), 0)
        t = tq(qb)
        ok = (kpos <= t) & (kpos > t - WINDOW)
        return online(s, ok, vwt_ref[0, 0, :, pl.ds(k0, qb)], carry)

    _, l_w, acc_w = lax.fori_loop(jnp.maximum(qi - WINDOW // qb, 0), qi + 1, win_step, init)

    o_s = acc_s * (1.0 / jnp.where(l_s > 0.0, l_s, 1.0))
    o_w = acc_w * (1.0 / jnp.where(l_w > 0.0, l_w, 1.0))
    g_c = gt_ref[0, pl.ds(hk * ng, ng), :]
    g_s = gt_ref[0, pl.ds(NSA_HEADS + hk * ng, ng), :]
    g_w = gt_ref[0, pl.ds(2 * NSA_HEADS + hk * ng, ng), :]
    for g in range(ng):
        sl = slice(g * qb, (g + 1) * qb)
        mix = g_c[g:g + 1, :] * o_c[:, sl] + g_s[g:g + 1, :] * o_s[:, sl] + g_w[g:g + 1, :] * o_w[:, sl]
        o_ref[0, g * HEAD_DIM:(g + 1) * HEAD_DIM, :] = mix.astype(BF16)


def _attention(qn, qr, gt, kc, vct, ks, vst, kw, vwt, ovl_t, batch, seq):
    n_cmp = seq // CMP_STRIDE
    n_sel = seq // SEL_BLOCK
    qb = Q_TILE
    n_gate = N_BRANCH * NSA_HEADS
    q_spec = pl.BlockSpec((1, NSA_GROUP, qb, HEAD_DIM), lambda b, hk, i: (b, hk, i, 0))
    nat_full = pl.BlockSpec((1, 1, seq, HEAD_DIM), lambda b, hk, i: (b, hk, 0, 0))
    tr_full = pl.BlockSpec((1, 1, HEAD_DIM, seq), lambda b, hk, i: (b, hk, 0, 0))
    return pl.pallas_call(
        functools.partial(_attn_kernel, n_cmp=n_cmp, n_sel=n_sel),
        grid=(batch, NSA_KV_HEADS, seq // qb),
        in_specs=[
            q_spec, q_spec,
            pl.BlockSpec((1, n_gate, qb), lambda b, hk, i: (b, 0, i)),
            pl.BlockSpec((1, 1, n_cmp, HEAD_DIM), lambda b, hk, i: (b, hk, 0, 0)),
            pl.BlockSpec((1, 1, HEAD_DIM, n_cmp), lambda b, hk, i: (b, hk, 0, 0)),
            nat_full, tr_full, nat_full, tr_full,
            _const_spec((n_sel, n_cmp)),
        ],
        out_specs=pl.BlockSpec((1, NSA_GROUP * HEAD_DIM, qb), lambda b, hk, i: (b, hk, i)),
        out_shape=jax.ShapeDtypeStruct((batch, D_MODEL, seq), BF16),
        scratch_shapes=[pltpu.VMEM((n_sel, qb), F32)],
        compiler_params=_params(("arbitrary", "arbitrary", "arbitrary")),
        name="nsa_attention",
    )(qn, qr, gt, kc, vct, ks, vst, kw, vwt, ovl_t)


def _out_kernel(x_ref, mod_ref, ot_ref, w_ref, o_ref):
    y = _dot_tn(ot_ref[0], w_ref[...])
    o_ref[...] = x_ref[...] + mod_ref[0][2:3] * y


def _out_proj(x2, mod, o_t, w_out, batch, seq):
    per_b = seq // ROW_TILE
    return pl.pallas_call(
        _out_kernel,
        grid=(batch, per_b),
        in_specs=[
            pl.BlockSpec((ROW_TILE, D_MODEL), lambda b, j: (b * per_b + j, 0)),
            pl.BlockSpec((1, 3, D_MODEL), lambda b, j: (b, 0, 0)),
            pl.BlockSpec((1, D_MODEL, ROW_TILE), lambda b, j: (b, 0, j)),
            _const_spec((D_MODEL, D_MODEL)),
        ],
        out_specs=pl.BlockSpec((ROW_TILE, D_MODEL), lambda b, j: (b * per_b + j, 0)),
        out_shape=jax.ShapeDtypeStruct(x2.shape, F32),
        compiler_params=_params(("arbitrary", "arbitrary")),
        name="nsa_out",
    )(x2, mod, o_t, w_out)


def _rope_lane_tables(seq):
    inv_freq = 1.0 / (ROPE_THETA ** (jnp.arange(0, HEAD_DIM, 2, dtype=F32) / HEAD_DIM))
    ang = jnp.arange(seq, dtype=F32)[:, None] * inv_freq[None, :]
    cos, sin = jnp.cos(ang), jnp.sin(ang)
    return jnp.tile(cos, (1, 4)), jnp.concatenate([-sin, sin, -sin, sin], axis=-1)


def _overlap_t(n_sel, n_cmp):
    cs = jnp.arange(n_cmp)[None, :] * CMP_STRIDE
    ss = jnp.arange(n_sel)[:, None] * SEL_BLOCK
    return ((cs < ss + SEL_BLOCK) & (cs + CMP_BLOCK > ss)).astype(BF16)


def kernel(x, c, ada_w, ada_b, norm_mix, norm_ffn, hgrn_w_in, hgrn_lower_bounds, hgrn_out_norm, hgrn_w_out,
           kv_ada_w, kv_ada_b, kv_norm, nsa_w_kv, nsa_k_norm, cmp_pos, cmp_w1, cmp_w2, nsa_w_q, nsa_q_norm,
           nsa_w_out, ffn_w_in, ffn_w_out):
    batch, seq, _ = x.shape
    w = NSA_KV_HEADS * HEAD_DIM
    x2 = x.reshape(batch * seq, D_MODEL)

    c_pad = jnp.zeros((8, D_MODEL), F32).at[:batch].set(c)
    mods = _mods(c_pad, ada_w, ada_b)[:, :batch].reshape(2 * DEPTH, batch, 3, D_MODEL)
    kv_mod = _mods(c_pad, kv_ada_w[None], kv_ada_b[None])[0, :batch].reshape(batch, 2, D_MODEL)

    cosn, sinn = _rope_lane_tables(seq)
    eye_seg = jnp.kron(jnp.eye(w // HEAD_DIM, dtype=F32), jnp.ones((HEAD_DIM, HEAD_DIM), F32)).astype(BF16)
    n_rows = seq // CMP_STRIDE
    ovl_t = _overlap_t(seq // SEL_BLOCK, n_rows)
    shared = None

    for layer in range(DEPTH):
        mix_mod = mods[2 * layer]
        gain = norm_mix[layer][None, :]
        if layer < N_A_LAYERS:
            x2 = _hgrn(x2, mix_mod, gain, hgrn_w_in[layer].astype(BF16), hgrn_lower_bounds,
                       hgrn_out_norm[layer][None, :], hgrn_w_out[layer].astype(BF16), batch, seq, layer)
        else:
            if shared is None:
                wkv = nsa_w_kv.reshape(D_MODEL, 2 * N_BRANCH, w)
                w_nat = jnp.concatenate([wkv[:, 0], wkv[:, 1], wkv[:, 2], wkv[:, 4]], axis=1).astype(BF16)
                w_vt = jnp.concatenate([wkv[:, 3], wkv[:, 5]], axis=1).T.astype(BF16)
                kcmp, vcmp, ks, kw, vst, vwt = _kv_prep(
                    x2, kv_mod, kv_norm[None, :], w_nat, w_vt, eye_seg,
                    jnp.tile(nsa_k_norm[1], NSA_KV_HEADS)[None, :], jnp.tile(nsa_k_norm[2], NSA_KV_HEADS)[None, :],
                    cosn, sinn, batch, seq)
                pos8 = jnp.zeros((2, 8, CMP_BLOCK * HEAD_DIM), F32).at[:, 0].set(
                    cmp_pos.reshape(2, CMP_BLOCK * HEAD_DIM))
                rk = kcmp.reshape(batch, NSA_KV_HEADS, n_rows, CMP_STRIDE * HEAD_DIM)
                rv = vcmp.reshape(batch, NSA_KV_HEADS, n_rows, CMP_STRIDE * HEAD_DIM)
                kc = _compress(rk, cmp_w1[0].astype(BF16), pos8[0], cmp_w2[0].astype(BF16),
                               nsa_k_norm[0][None, :], batch, n_rows, False)
                vct = _compress(rv, cmp_w1[1].astype(BF16), pos8[1], cmp_w2[1].T.astype(BF16),
                                None, batch, n_rows, True)
                shared = (kc, vct, ks, vst, kw, vwt)
            bl = layer - N_A_LAYERS
            wq = nsa_w_q[bl]
            qn, qr, gt = _q_proj(x2, mix_mod, gain, wq[:, :D_MODEL].astype(BF16), wq[:, D_MODEL:].T.astype(BF16),
                                 eye_seg, jnp.tile(nsa_q_norm[bl], NSA_GROUP)[None, :], cosn, sinn, batch, seq)
            o_t = _attention(qn, qr, gt, *shared, ovl_t, batch, seq)
            x2 = _out_proj(x2, mix_mod, o_t, nsa_w_out[bl].astype(BF16), batch, seq)
        x2 = _ffn(x2, mods[2 * layer + 1], norm_ffn[layer][None, :], ffn_w_in[layer].astype(BF16),
                  ffn_w_out[layer].astype(BF16), seq)
    return x2.reshape(batch, seq, D_MODEL)
```

```python
import functools

import jax
import jax.numpy as jnp
from jax import lax
from jax.experimental import pallas as pl
from jax.experimental.pallas import tpu as pltpu

F32 = jnp.float32
BF16 = jnp.bfloat16

D_MODEL = 1024
DEPTH = 4
N_A_LAYERS = DEPTH // 2
HGRN_HEADS = 8
HGRN_DK = 128
NSA_HEADS = 16
NSA_KV_HEADS = 4
NSA_GROUP = 4
HEAD_DIM = 64
N_BRANCH = 3
CMP_BLOCK = 32
CMP_STRIDE = 16
CMP_HIDDEN = 256
SEL_BLOCK = 64
N_SELECT = 16
WINDOW = 512
ROPE_THETA = 10000.0
FFN_HIDDEN = 2816
EPS = 1e-6
NEG_INF = -1e30
FORCE_SCORE = 1e9

LANES = 128
MXU_N = 256
VMEM_LIMIT = 56 * 1024 * 1024

ROW_TILE = 512
HGRN_CHUNK = 128
Q_TILE = 128
SEL_KEYS = 256
SEL_BIG = 1024
FFN_CHUNK = 256
V_ROWS = HEAD_DIM + 16
LOG2E = 1.4426950408889634


def _params(sem):
    return pltpu.CompilerParams(dimension_semantics=sem, vmem_limit_bytes=VMEM_LIMIT)


def _const_spec(shape):
    nd = len(shape)
    return pl.BlockSpec(shape, lambda *_: (0,) * nd, pipeline_mode=pl.Buffered(1))


def _dot(a, b):
    return jnp.dot(a, b, preferred_element_type=F32)


def _dot_nt(a, b):
    return lax.dot_general(a, b, (((1,), (1,)), ((), ())), preferred_element_type=F32)


def _dot_tn(a, b):
    return lax.dot_general(a, b, (((0,), (0,)), ((), ())), preferred_element_type=F32)


def _split(a):
    hi = a.astype(BF16)
    lo = (a - hi.astype(F32)).astype(BF16)
    return hi, lo


def _sigmoid(x):
    return 1.0 / (1.0 + jnp.exp(-x))


def _silu(x):
    return x * _sigmoid(x)


def _mod_norm(x, gain, shift, scale):
    ms = jnp.mean(x * x, axis=-1, keepdims=True)
    y = x * lax.rsqrt(ms + EPS) * gain
    return y * (1.0 + scale) + shift


def _mods_kernel(c_ref, w_ref, b_ref, o_ref):
    c = c_ref[...]
    ah, al = _split(_silu(c))
    wh, wl = _split(w_ref[0])
    o_ref[0] = _dot(ah, wh) + _dot(al, wh) + _dot(ah, wl) + b_ref[0]


def _mods(c_pad, w, b):
    n_l, _, n = w.shape
    tn = 1024
    return pl.pallas_call(
        _mods_kernel,
        grid=(n_l, n // tn),
        in_specs=[
            pl.BlockSpec((8, D_MODEL), lambda l, j: (0, 0)),
            pl.BlockSpec((1, D_MODEL, tn), lambda l, j: (l, 0, j)),
            pl.BlockSpec((1, 1, tn), lambda l, j: (l, 0, j)),
        ],
        out_specs=pl.BlockSpec((1, 8, tn), lambda l, j: (l, 0, j)),
        out_shape=jax.ShapeDtypeStruct((n_l, 8, n), F32),
        compiler_params=_params(("arbitrary", "arbitrary")),
        name="adaln_mods",
    )(c_pad, w, b.reshape(n_l, 1, n))


def _ffn_kernel(x_ref, mod_ref, gain_ref, win_ref, wout_ref, o_ref):
    x = x_ref[...]
    m = mod_ref[0]
    h = _mod_norm(x, gain_ref[...], m[0:1], m[1:2]).astype(BF16)
    acc = jnp.zeros(x.shape, F32)
    for c in range(FFN_HIDDEN // FFN_CHUNK):
        lo = c * FFN_CHUNK
        a = _dot(h, win_ref[:, lo:lo + FFN_CHUNK])
        b = _dot(h, win_ref[:, FFN_HIDDEN + lo:FFN_HIDDEN + lo + FFN_CHUNK])
        g = (_silu(a) * b).astype(BF16)
        acc = acc + _dot(g, wout_ref[lo:lo + FFN_CHUNK, :])
    o_ref[...] = x + m[2:3] * acc


def _ffn(x2, mod, gain, w_in, w_out, seq):
    t = x2.shape[0]
    per_b = seq // ROW_TILE
    return pl.pallas_call(
        _ffn_kernel,
        grid=(t // ROW_TILE,),
        in_specs=[
            pl.BlockSpec((ROW_TILE, D_MODEL), lambda i: (i, 0)),
            pl.BlockSpec((1, 3, D_MODEL), lambda i: (i // per_b, 0, 0)),
            _const_spec((1, D_MODEL)),
            _const_spec((D_MODEL, 2 * FFN_HIDDEN)),
            _const_spec((FFN_HIDDEN, D_MODEL)),
        ],
        out_specs=pl.BlockSpec((ROW_TILE, D_MODEL), lambda i: (i, 0)),
        out_shape=jax.ShapeDtypeStruct(x2.shape, F32),
        compiler_params=_params(("arbitrary",)),
        name="ffn",
    )(x2, mod, gain, w_in, w_out)


def _hgrn_kernel(x_ref, mod_ref, gain_ref, win_ref, lbraw_ref, onorm_ref, wout_ref,
                 o_ref, proj_scr, st_scr, oall_scr, *, layer):
    tc = HGRN_CHUNK
    dk = HGRN_DK

    @pl.when(pl.program_id(1) == 0)
    def _():
        st_scr[...] = jnp.zeros(st_scr.shape, F32)

    x = x_ref[...]
    m = mod_ref[0]
    h = _mod_norm(x, gain_ref[...], m[0:1], m[1:2]).astype(BF16)
    proj_scr[...] = _dot(h, win_ref[...])

    if layer > 0:
        raw = lbraw_ref[...]
        e = jnp.exp(raw - jnp.max(raw, axis=0, keepdims=True))
        sm = e / jnp.sum(e, axis=0, keepdims=True)
        lb = jnp.sum(sm[1:layer + 1], axis=0, keepdims=True)

    row = lax.broadcasted_iota(jnp.int32, (tc, D_MODEL), 0)
    r_i = lax.broadcasted_iota(jnp.int32, (tc, tc), 0)
    c_i = lax.broadcasted_iota(jnp.int32, (tc, tc), 1)
    ones_b = jnp.ones((dk, dk), BF16)
    n_lvl = tc.bit_length() - 1

    def chunk(ci, carry):
        r0 = pl.multiple_of(ci * tc, tc)
        qp = proj_scr[pl.ds(r0, tc), 0:D_MODEL]
        fp = proj_scr[pl.ds(r0, tc), D_MODEL:2 * D_MODEL]
        v = proj_scr[pl.ds(r0, tc), 2 * D_MODEL:3 * D_MODEL].astype(BF16)
        gp = proj_scr[pl.ds(r0, tc), 3 * D_MODEL:4 * D_MODEL]

        q = _silu(qp)
        e = jnp.exp(-jnp.abs(fp))
        r = 1.0 / (1.0 + e)
        pos = fp >= 0.0
        sig = jnp.where(pos, r, e * r)
        nsig = jnp.where(pos, e * r, r)
        if layer == 0:
            logf = jnp.minimum(fp, 0.0) - jnp.log(1.0 + e)
            kk = nsig
        else:
            logf = jnp.log(lb + (1.0 - lb) * sig)
            kk = (1.0 - lb) * nsig

        g_cum = logf
        s = 1
        while s < tc:
            g_cum = g_cum + jnp.where(row >= s, pltpu.roll(g_cum, s, 0), 0.0)
            s *= 2

        p_acc = [jnp.zeros((tc, tc), F32) for _ in range(HGRN_HEADS)]
        end_val = g_cum
        for lvl in range(n_lvl):
            hs = 1 << lvl
            second = (row & hs) != 0
            ref_val = jnp.where(second, pltpu.roll(end_val, hs, 0), end_val)
            decay = jnp.exp(-jnp.abs(g_cum - ref_val))
            qs = jnp.where(second, q * decay, 0.0).astype(BF16)
            ks = jnp.where(second, 0.0, kk * decay).astype(BF16)
            same = (r_i >> (lvl + 1)) == (c_i >> (lvl + 1))
            for hd in range(HGRN_HEADS):
                sl = slice(hd * dk, (hd + 1) * dk)
                p_acc[hd] = p_acc[hd] + jnp.where(same, _dot_nt(qs[:, sl], ks[:, sl]), 0.0)
            if lvl + 1 < n_lvl:
                end_val = jnp.where(second, end_val, pltpu.roll(end_val, tc - hs, 0))

        qk = (q * kk).astype(BF16)
        g_last = g_cum[tc - 1:tc, :]
        qe = (q * jnp.exp(g_cum)).astype(BF16)
        kd = (kk * jnp.exp(g_last - g_cum)).astype(BF16)
        s_decay = jnp.exp(g_last)
        gate_act = _silu(gp)
        onorm = onorm_ref[...]
        for hd in range(HGRN_HEADS):
            sl = slice(hd * dk, (hd + 1) * dk)
            p_h = p_acc[hd] + jnp.where(r_i == c_i, _dot(qk[:, sl], ones_b), 0.0)
            st = st_scr[hd]
            o_h = _dot(p_h.astype(BF16), v[:, sl]) + _dot_nt(qe[:, sl], st.astype(BF16))
            st_scr[hd] = s_decay[:, sl] * st + _dot_tn(v[:, sl], kd[:, sl])
            ms = jnp.mean(o_h * o_h, axis=-1, keepdims=True)
            o_n = o_h * lax.rsqrt(ms + EPS) * onorm
            oall_scr[pl.ds(r0, tc), sl] = (o_n * gate_act[:, sl]).astype(BF16)
        return carry

    lax.fori_loop(0, ROW_TILE // tc, chunk, 0)
    o_ref[...] = x + m[2:3] * _dot(oall_scr[...], wout_ref[...])


def _hgrn(x2, mod, gain, w_in, lb_raw, onorm, w_out, batch, seq, layer):
    per_b = seq // ROW_TILE
    return pl.pallas_call(
        functools.partial(_hgrn_kernel, layer=layer),
        grid=(batch, per_b),
        in_specs=[
            pl.BlockSpec((ROW_TILE, D_MODEL), lambda b, j: (b * per_b + j, 0)),
            pl.BlockSpec((1, 3, D_MODEL), lambda b, j: (b, 0, 0)),
            _const_spec((1, D_MODEL)),
            _const_spec((D_MODEL, 4 * D_MODEL)),
            _const_spec((N_A_LAYERS, D_MODEL)),
            _const_spec((1, HGRN_DK)),
            _const_spec((D_MODEL, D_MODEL)),
        ],
        out_specs=pl.BlockSpec((ROW_TILE, D_MODEL), lambda b, j: (b * per_b + j, 0)),
        out_shape=jax.ShapeDtypeStruct(x2.shape, F32),
        scratch_shapes=[
            pltpu.VMEM((ROW_TILE, 4 * D_MODEL), F32),
            pltpu.VMEM((HGRN_HEADS, HGRN_DK, HGRN_DK), F32),
            pltpu.VMEM((ROW_TILE, D_MODEL), BF16),
        ],
        compiler_params=_params(("arbitrary", "arbitrary")),
        name=f"hgrn{layer}",
    )(x2, mod, gain, w_in, lb_raw, onorm, w_out)


def _segnorm64(xc, seg_ones, gain):
    hi, lo = _split(xc * xc)
    ss = _dot(hi, seg_ones) + _dot(lo, seg_ones)
    return xc * lax.rsqrt(ss * (1.0 / HEAD_DIM) + EPS) * gain


def _rope_lanes(xp, cosn, sinn):
    lane = lax.broadcasted_iota(jnp.int32, xp.shape, 1)
    first = (lane & (HEAD_DIM // 2)) == 0
    rot = jnp.where(first, pltpu.roll(xp, LANES - HEAD_DIM // 2, 1), pltpu.roll(xp, HEAD_DIM // 2, 1))
    return xp * cosn + rot * sinn


def _rope_wide(xc, cosn, sinn):
    return jnp.concatenate(
        [_rope_lanes(xc[:, i * LANES:(i + 1) * LANES], cosn, sinn) for i in range(xc.shape[1] // LANES)], axis=1)


def _kv_kernel(x_ref, mod_ref, gain_ref, wn_ref, wvt_ref, seg_ref, gsel_ref, gwin_ref, cos_ref, sin_ref,
               kcmp_ref, vcmp_ref, ksel_ref, kwin_ref, vselt_ref, vwint_ref):
    x = x_ref[...]
    m = mod_ref[0]
    h = _mod_norm(x, gain_ref[...], m[0:1], m[1:2]).astype(BF16)
    nat = _dot(h, wn_ref[...])
    vt = _dot_nt(wvt_ref[...], h)
    w = NSA_KV_HEADS * HEAD_DIM
    seg = seg_ref[...]
    cosn = cos_ref[...]
    sinn = sin_ref[...]
    kcmp = nat[:, 0:w].astype(BF16)
    vcmp = nat[:, w:2 * w].astype(BF16)
    ksel = _rope_wide(_segnorm64(nat[:, 2 * w:3 * w], seg, gsel_ref[...]), cosn, sinn).astype(BF16)
    kwin = _rope_wide(_segnorm64(nat[:, 3 * w:4 * w], seg, gwin_ref[...]), cosn, sinn).astype(BF16)
    ones = jnp.ones((V_ROWS - HEAD_DIM, x.shape[0]), BF16)
    for hd in range(NSA_KV_HEADS):
        sl = slice(hd * HEAD_DIM, (hd + 1) * HEAD_DIM)
        kcmp_ref[0, hd] = kcmp[:, sl]
        vcmp_ref[0, hd] = vcmp[:, sl]
        ksel_ref[0, hd] = ksel[:, sl]
        kwin_ref[0, hd] = kwin[:, sl]
        vselt_ref[0, hd, 0:HEAD_DIM, :] = vt[hd * HEAD_DIM:(hd + 1) * HEAD_DIM, :].astype(BF16)
        vselt_ref[0, hd, HEAD_DIM:V_ROWS, :] = ones
        vwint_ref[0, hd, 0:HEAD_DIM, :] = vt[w + hd * HEAD_DIM:w + (hd + 1) * HEAD_DIM, :].astype(BF16)
        vwint_ref[0, hd, HEAD_DIM:V_ROWS, :] = ones


def _kv_prep(x2, mod, gain, w_nat, w_vt, seg, gsel, gwin, cosn, sinn, batch, seq):
    per_b = seq // ROW_TILE
    w = NSA_KV_HEADS * HEAD_DIM
    nat_spec = pl.BlockSpec((1, NSA_KV_HEADS, ROW_TILE, HEAD_DIM), lambda b, j: (b, 0, j, 0))
    tr_spec = pl.BlockSpec((1, NSA_KV_HEADS, V_ROWS, ROW_TILE), lambda b, j: (b, 0, 0, j))
    nat_shape = jax.ShapeDtypeStruct((batch, NSA_KV_HEADS, seq, HEAD_DIM), BF16)
    tr_shape = jax.ShapeDtypeStruct((batch, NSA_KV_HEADS, V_ROWS, seq), BF16)
    return pl.pallas_call(
        _kv_kernel,
        grid=(batch, per_b),
        in_specs=[
            pl.BlockSpec((ROW_TILE, D_MODEL), lambda b, j: (b * per_b + j, 0)),
            pl.BlockSpec((1, 2, D_MODEL), lambda b, j: (b, 0, 0)),
            _const_spec((1, D_MODEL)),
            _const_spec((D_MODEL, 4 * w)),
            _const_spec((2 * w, D_MODEL)),
            _const_spec((w, w)),
            _const_spec((1, w)),
            _const_spec((1, w)),
            pl.BlockSpec((ROW_TILE, LANES), lambda b, j: (j, 0)),
            pl.BlockSpec((ROW_TILE, LANES), lambda b, j: (j, 0)),
        ],
        out_specs=[nat_spec, nat_spec, nat_spec, nat_spec, tr_spec, tr_spec],
        out_shape=[nat_shape, nat_shape, nat_shape, nat_shape, tr_shape, tr_shape],
        compiler_params=_params(("arbitrary", "arbitrary")),
        name="nsa_kv",
    )(x2, mod, gain, w_nat, w_vt, seg, gsel, gwin, cosn, sinn)


def _compress_pre(r_ref, w1_ref, pos_ref):
    r = r_ref[0, 0]
    half = CMP_STRIDE * HEAD_DIM
    w1a = w1_ref[0:half, :]
    w1b = w1_ref[half:2 * half, :]
    n_rows = r.shape[0]
    u = _dot(r, w1a)
    v = _dot(r, w1b)
    pos = pos_ref[...]
    ph, pl_ = _split(pos)
    bias = (_dot(ph[:, 0:half], w1a) + _dot(pl_[:, 0:half], w1a)
            + _dot(ph[:, half:], w1b) + _dot(pl_[:, half:], w1b))[0:1, :]
    pre = u + pltpu.roll(v, n_rows - 1, 0) + bias
    return _silu(pre).astype(BF16)


def _compress_k_kernel(r_ref, w1_ref, pos_ref, w2_ref, gain_ref, o_ref):
    hid = _compress_pre(r_ref, w1_ref, pos_ref)
    out = _dot(hid, w2_ref[...])
    ms = jnp.mean(out * out, axis=-1, keepdims=True)
    o_ref[0, 0] = (out * lax.rsqrt(ms + EPS) * gain_ref[...]).astype(BF16)


def _compress_v_kernel(r_ref, w1_ref, pos_ref, w2t_ref, o_ref):
    hid = _compress_pre(r_ref, w1_ref, pos_ref)
    o_ref[0, 0] = _dot_nt(w2t_ref[...], hid).astype(BF16)


def _compress(r, w1, pos8, w2, gain, batch, n_rows, transposed):
    half2 = CMP_BLOCK * HEAD_DIM
    in_specs = [
        pl.BlockSpec((1, 1, n_rows, CMP_STRIDE * HEAD_DIM), lambda b, hd: (b, hd, 0, 0)),
        _const_spec((half2, CMP_HIDDEN)),
        _const_spec((8, half2)),
    ]
    if transposed:
        kern = _compress_v_kernel
        in_specs.append(_const_spec((HEAD_DIM, CMP_HIDDEN)))
        args = (r, w1, pos8, w2)
        out_spec = pl.BlockSpec((1, 1, HEAD_DIM, n_rows), lambda b, hd: (b, hd, 0, 0))
        out_shape = jax.ShapeDtypeStruct((batch, NSA_KV_HEADS, HEAD_DIM, n_rows), BF16)
    else:
        kern = _compress_k_kernel
        in_specs += [_const_spec((CMP_HIDDEN, HEAD_DIM)), _const_spec((1, HEAD_DIM))]
        args = (r, w1, pos8, w2, gain)
        out_spec = pl.BlockSpec((1, 1, n_rows, HEAD_DIM), lambda b, hd: (b, hd, 0, 0))
        out_shape = jax.ShapeDtypeStruct((batch, NSA_KV_HEADS, n_rows, HEAD_DIM), BF16)
    return pl.pallas_call(
        kern,
        grid=(batch, NSA_KV_HEADS),
        in_specs=in_specs,
        out_specs=out_spec,
        out_shape=out_shape,
        compiler_params=_params(("arbitrary", "arbitrary")),
        name="nsa_compress_v" if transposed else "nsa_compress_k",
    )(*args)


def _q_kernel(x_ref, mod_ref, gain_ref, wq_ref, wgt_ref, seg_ref, qgain_ref, cos_ref, sin_ref,
              qn_ref, qr_ref, gt_ref):
    x = x_ref[...]
    m = mod_ref[0]
    h = _mod_norm(x, gain_ref[...], m[0:1], m[1:2]).astype(BF16)
    gt_ref[0] = _sigmoid(_dot_nt(wgt_ref[...], h))
    seg = seg_ref[...]
    cosn = cos_ref[...]
    sinn = sin_ref[...]
    scale = HEAD_DIM ** -0.5 * LOG2E
    w = NSA_GROUP * HEAD_DIM
    for c in range(NSA_KV_HEADS):
        qc = _dot(h, wq_ref[:, c * w:(c + 1) * w])
        qn = _segnorm64(qc, seg, qgain_ref[...]) * scale
        qr = _rope_wide(qn, cosn, sinn)
        qn = qn.astype(BF16)
        qr = qr.astype(BF16)
        for g in range(NSA_GROUP):
            sl = slice(g * HEAD_DIM, (g + 1) * HEAD_DIM)
            qn_ref[0, c * NSA_GROUP + g] = qn[:, sl]
            qr_ref[0, c * NSA_GROUP + g] = qr[:, sl]


def _q_proj(x2, mod, gain, w_q, w_gt, seg, qgain, cosn, sinn, batch, seq):
    per_b = seq // ROW_TILE
    w = NSA_GROUP * HEAD_DIM
    n_gate = N_BRANCH * NSA_HEADS
    q_spec = pl.BlockSpec((1, NSA_HEADS, ROW_TILE, HEAD_DIM), lambda b, j: (b, 0, j, 0))
    q_shape = jax.ShapeDtypeStruct((batch, NSA_HEADS, seq, HEAD_DIM), BF16)
    return pl.pallas_call(
        _q_kernel,
        grid=(batch, per_b),
        in_specs=[
            pl.BlockSpec((ROW_TILE, D_MODEL), lambda b, j: (b * per_b + j, 0)),
            pl.BlockSpec((1, 3, D_MODEL), lambda b, j: (b, 0, 0)),
            _const_spec((1, D_MODEL)),
            _const_spec((D_MODEL, D_MODEL)),
            _const_spec((n_gate, D_MODEL)),
            _const_spec((w, w)),
            _const_spec((1, w)),
            pl.BlockSpec((ROW_TILE, LANES), lambda b, j: (j, 0)),
            pl.BlockSpec((ROW_TILE, LANES), lambda b, j: (j, 0)),
        ],
        out_specs=[q_spec, q_spec, pl.BlockSpec((1, n_gate, ROW_TILE), lambda b, j: (b, 0, j))],
        out_shape=[q_shape, q_shape, jax.ShapeDtypeStruct((batch, n_gate, seq), F32)],
        compiler_params=_params(("arbitrary", "arbitrary")),
        name="nsa_q",
    )(x2, mod, gain, w_q, w_gt, seg, qgain, cosn, sinn)


def _attn_kernel(qn_ref, qr_ref, gt_ref, kc_ref, vct_ref, ks_ref, vst_ref, kw_ref, vwt_ref, ovl_ref,
                 o_ref, sel_scr, *, n_cmp, n_sel):
    qb = Q_TILE
    ng = NSA_GROUP
    hk = pl.program_id(1)
    qi = pl.program_id(2)
    t0 = qi * qb
    q_n = qn_ref[0].reshape(ng * qb, HEAD_DIM)
    q_r = qr_ref[0].reshape(ng * qb, HEAD_DIM)

    def tq(rows, cols):
        lane = lax.broadcasted_iota(jnp.int32, (rows, cols), 1)
        return t0 + (lane & (qb - 1))

    s_c = _dot_nt(kc_ref[0, 0], q_n)
    c_end = lax.broadcasted_iota(jnp.int32, (n_cmp, ng * qb), 0) * CMP_STRIDE + (CMP_BLOCK - 1)
    ok_c = c_end <= tq(n_cmp, ng * qb)
    s_c = jnp.where(ok_c, s_c, NEG_INF)
    e_c = jnp.where(ok_c, jnp.exp2(s_c - jnp.max(s_c, axis=0, keepdims=True)), 0.0)
    l_c = jnp.sum(e_c, axis=0, keepdims=True)
    p_c = e_c * (1.0 / jnp.where(l_c > 0.0, l_c, 1.0))
    o_c = _dot(vct_ref[0, 0], p_c.astype(BF16))

    p_sum = p_c[:, 0:qb]
    for g in range(1, ng):
        p_sum = p_sum + p_c[:, g * qb:(g + 1) * qb]
    ph, pl_ = _split(p_sum)
    imp = _dot(ovl_ref[...], ph) + _dot(ovl_ref[...], pl_)
    j_i = lax.broadcasted_iota(jnp.int32, (n_sel, qb), 0)
    cur = (t0 + lax.broadcasted_iota(jnp.int32, (n_sel, qb), 1)) // SEL_BLOCK
    valid = j_i <= cur
    forced = (j_i == 0) | (j_i == cur) | (j_i == cur - 1)
    val = jnp.where(forced & valid, FORCE_SCORE, jnp.where(valid, imp, -1.0))
    sel_bias = jnp.full((n_sel, qb), NEG_INF, F32)
    for _ in range(min(N_SELECT, n_sel)):
        mx = jnp.max(val, axis=0, keepdims=True)
        idx = jnp.min(jnp.where(val == mx, j_i, n_sel), axis=0, keepdims=True)
        hit = j_i == idx
        sel_bias = jnp.where(hit & (mx >= 0.0), 0.0, sel_bias)
        val = jnp.where(hit, -3.0e38, val)
    sel_scr[...] = sel_bias

    def bias_rows(j0, n_blk):
        rows = sel_scr[pl.ds(j0, n_blk), :]
        return jnp.concatenate(
            [jnp.broadcast_to(rows[i:i + 1, :], (SEL_BLOCK, qb)) for i in range(n_blk)], axis=0)

    def soft_step(s, bias, v_aug, carry):
        m_old, acc = carry
        sb = s + jnp.concatenate([bias] * ng, axis=1)
        m_new = jnp.maximum(m_old, jnp.max(sb, axis=0, keepdims=True))
        p = jnp.exp2(sb - m_new).astype(BF16)
        return m_new, jnp.exp2(m_old - m_new) * acc + _dot(v_aug, p)

    init = (jnp.full((1, ng * qb), NEG_INF, F32), jnp.zeros((V_ROWS, ng * qb), F32))

    def big_step(kt, carry):
        k0 = pl.multiple_of(kt * SEL_BIG, SEL_BIG)
        s = _dot_nt(ks_ref[0, 0, pl.ds(k0, SEL_BIG), :], q_r)
        bias = bias_rows(kt * (SEL_BIG // SEL_BLOCK), SEL_BIG // SEL_BLOCK)
        return soft_step(s, bias, vst_ref[0, 0, :, pl.ds(k0, SEL_BIG)], carry)

    def tail_step(kt, carry):
        k0 = pl.multiple_of(kt * SEL_KEYS, SEL_KEYS)
        s = _dot_nt(ks_ref[0, 0, pl.ds(k0, SEL_KEYS), :], q_r)
        kpos = k0 + lax.broadcasted_iota(jnp.int32, (SEL_KEYS, qb), 0)
        bias = jnp.where(kpos <= tq(SEL_KEYS, qb), bias_rows(kt * (SEL_KEYS // SEL_BLOCK), SEL_KEYS // SEL_BLOCK),
                         NEG_INF)
        return soft_step(s, bias, vst_ref[0, 0, :, pl.ds(k0, SEL_KEYS)], carry)

    n_big = t0 // SEL_BIG
    carry = lax.fori_loop(0, n_big, big_step, init)
    _, acc_s = lax.fori_loop(n_big * (SEL_BIG // SEL_KEYS), (t0 + qb + SEL_KEYS - 1) // SEL_KEYS, tail_step, carry)

    n_win = WINDOW + qb
    w0 = pl.multiple_of(jnp.maximum(t0 - WINDOW, 0), qb)
    s_w = _dot_nt(kw_ref[0, 0, pl.ds(w0, n_win), :], q_r)
    kpos = w0 + lax.broadcasted_iota(jnp.int32, (n_win, qb), 0)
    t_w = tq(n_win, qb)
    bias_w = jnp.where((kpos <= t_w) & (kpos > t_w - WINDOW), 0.0, NEG_INF)
    sb_w = s_w + jnp.concatenate([bias_w] * ng, axis=1)
    p_w = jnp.exp2(sb_w - jnp.max(sb_w, axis=0, keepdims=True)).astype(BF16)
    acc_w = _dot(vwt_ref[0, 0, :, pl.ds(w0, n_win)], p_w)

    o_s = acc_s[0:HEAD_DIM] * (1.0 / acc_s[HEAD_DIM:HEAD_DIM + 1])
    o_w = acc_w[0:HEAD_DIM] * (1.0 / acc_w[HEAD_DIM:HEAD_DIM + 1])
    g_c = gt_ref[0, pl.ds(hk * ng, ng), :]
    g_s = gt_ref[0, pl.ds(NSA_HEADS + hk * ng, ng), :]
    g_w = gt_ref[0, pl.ds(2 * NSA_HEADS + hk * ng, ng), :]
    for g in range(ng):
        sl = slice(g * qb, (g + 1) * qb)
        mix = g_c[g:g + 1, :] * o_c[:, sl] + g_s[g:g + 1, :] * o_s[:, sl] + g_w[g:g + 1, :] * o_w[:, sl]
        o_ref[0, g * HEAD_DIM:(g + 1) * HEAD_DIM, :] = mix.astype(BF16)


def _attention(qn, qr, gt, kc, vct, ks, vst, kw, vwt, ovl_t, batch, seq):
    n_cmp = seq // CMP_STRIDE
    n_sel = seq // SEL_BLOCK
    qb = Q_TILE
    n_gate = N_BRANCH * NSA_HEADS
    q_spec = pl.BlockSpec((1, NSA_GROUP, qb, HEAD_DIM), lambda b, hk, i: (b, hk, i, 0))
    nat_full = pl.BlockSpec((1, 1, seq, HEAD_DIM), lambda b, hk, i: (b, hk, 0, 0))
    tr_full = pl.BlockSpec((1, 1, V_ROWS, seq), lambda b, hk, i: (b, hk, 0, 0))
    return pl.pallas_call(
        functools.partial(_attn_kernel, n_cmp=n_cmp, n_sel=n_sel),
        grid=(batch, NSA_KV_HEADS, seq // qb),
        in_specs=[
            q_spec, q_spec,
            pl.BlockSpec((1, n_gate, qb), lambda b, hk, i: (b, 0, i)),
            pl.BlockSpec((1, 1, n_cmp, HEAD_DIM), lambda b, hk, i: (b, hk, 0, 0)),
            pl.BlockSpec((1, 1, HEAD_DIM, n_cmp), lambda b, hk, i: (b, hk, 0, 0)),
            nat_full, tr_full, nat_full, tr_full,
            _const_spec((n_sel, n_cmp)),
        ],
        out_specs=pl.BlockSpec((1, NSA_GROUP * HEAD_DIM, qb), lambda b, hk, i: (b, hk, i)),
        out_shape=jax.ShapeDtypeStruct((batch, D_MODEL, seq), BF16),
        scratch_shapes=[pltpu.VMEM((n_sel, qb), F32)],
        compiler_params=_params(("arbitrary", "arbitrary", "arbitrary")),
        name="nsa_attention",
    )(qn, qr, gt, kc, vct, ks, vst, kw, vwt, ovl_t)


def _out_kernel(x_ref, mod_ref, ot_ref, w_ref, o_ref):
    y = _dot_tn(ot_ref[0], w_ref[...])
    o_ref[...] = x_ref[...] + mod_ref[0][2:3] * y


def _out_proj(x2, mod, o_t, w_out, batch, seq):
    per_b = seq // ROW_TILE
    return pl.pallas_call(
        _out_kernel,
        grid=(batch, per_b),
        in_specs=[
            pl.BlockSpec((ROW_TILE, D_MODEL), lambda b, j: (b * per_b + j, 0)),
            pl.BlockSpec((1, 3, D_MODEL), lambda b, j: (b, 0, 0)),
            pl.BlockSpec((1, D_MODEL, ROW_TILE), lambda b, j: (b, 0, j)),
            _const_spec((D_MODEL, D_MODEL)),
        ],
        out_specs=pl.BlockSpec((ROW_TILE, D_MODEL), lambda b, j: (b * per_b + j, 0)),
        out_shape=jax.ShapeDtypeStruct(x2.shape, F32),
        compiler_params=_params(("arbitrary", "arbitrary")),
        name="nsa_out",
    )(x2, mod, o_t, w_out)


def _rope_lane_tables(seq):
    inv_freq = 1.0 / (ROPE_THETA ** (jnp.arange(0, HEAD_DIM, 2, dtype=F32) / HEAD_DIM))
    ang = jnp.arange(seq, dtype=F32)[:, None] * inv_freq[None, :]
    cos, sin = jnp.cos(ang), jnp.sin(ang)
    return jnp.tile(cos, (1, 4)), jnp.concatenate([-sin, sin, -sin, sin], axis=-1)


def _overlap_t(n_sel, n_cmp):
    cs = jnp.arange(n_cmp)[None, :] * CMP_STRIDE
    ss = jnp.arange(n_sel)[:, None] * SEL_BLOCK
    return ((cs < ss + SEL_BLOCK) & (cs + CMP_BLOCK > ss)).astype(BF16)


def kernel(x, c, ada_w, ada_b, norm_mix, norm_ffn, hgrn_w_in, hgrn_lower_bounds, hgrn_out_norm, hgrn_w_out,
           kv_ada_w, kv_ada_b, kv_norm, nsa_w_kv, nsa_k_norm, cmp_pos, cmp_w1, cmp_w2, nsa_w_q, nsa_q_norm,
           nsa_w_out, ffn_w_in, ffn_w_out):
    batch, seq, _ = x.shape
    w = NSA_KV_HEADS * HEAD_DIM
    x2 = x.reshape(batch * seq, D_MODEL)

    c_pad = jnp.zeros((8, D_MODEL), F32).at[:batch].set(c)
    mods = _mods(c_pad, ada_w, ada_b)[:, :batch].reshape(2 * DEPTH, batch, 3, D_MODEL)
    kv_mod = _mods(c_pad, kv_ada_w[None], kv_ada_b[None])[0, :batch].reshape(batch, 2, D_MODEL)

    cosn, sinn = _rope_lane_tables(seq)
    eye_seg = jnp.kron(jnp.eye(w // HEAD_DIM, dtype=F32), jnp.ones((HEAD_DIM, HEAD_DIM), F32)).astype(BF16)
    n_rows = seq // CMP_STRIDE
    ovl_t = _overlap_t(seq // SEL_BLOCK, n_rows)
    shared = None

    for layer in range(DEPTH):
        mix_mod = mods[2 * layer]
        gain = norm_mix[layer][None, :]
        if layer < N_A_LAYERS:
            x2 = _hgrn(x2, mix_mod, gain, hgrn_w_in[layer].astype(BF16), hgrn_lower_bounds,
                       hgrn_out_norm[layer][None, :], hgrn_w_out[layer].astype(BF16), batch, seq, layer)
        else:
            if shared is None:
                wkv = nsa_w_kv.reshape(D_MODEL, 2 * N_BRANCH, w)
                w_nat = jnp.concatenate([wkv[:, 0], wkv[:, 1], wkv[:, 2], wkv[:, 4]], axis=1).astype(BF16)
                w_vt = jnp.concatenate([wkv[:, 3], wkv[:, 5]], axis=1).T.astype(BF16)
                kcmp, vcmp, ks, kw, vst, vwt = _kv_prep(
                    x2, kv_mod, kv_norm[None, :], w_nat, w_vt, eye_seg,
                    jnp.tile(nsa_k_norm[1], NSA_KV_HEADS)[None, :], jnp.tile(nsa_k_norm[2], NSA_KV_HEADS)[None, :],
                    cosn, sinn, batch, seq)
                pos8 = jnp.zeros((2, 8, CMP_BLOCK * HEAD_DIM), F32).at[:, 0].set(
                    cmp_pos.reshape(2, CMP_BLOCK * HEAD_DIM))
                rk = kcmp.reshape(batch, NSA_KV_HEADS, n_rows, CMP_STRIDE * HEAD_DIM)
                rv = vcmp.reshape(batch, NSA_KV_HEADS, n_rows, CMP_STRIDE * HEAD_DIM)
                kc = _compress(rk, cmp_w1[0].astype(BF16), pos8[0], cmp_w2[0].astype(BF16),
                               nsa_k_norm[0][None, :], batch, n_rows, False)
                vct = _compress(rv, cmp_w1[1].astype(BF16), pos8[1], cmp_w2[1].T.astype(BF16),
                                None, batch, n_rows, True)
                shared = (kc, vct, ks, vst, kw, vwt)
            bl = layer - N_A_LAYERS
            wq = nsa_w_q[bl]
            qn, qr, gt = _q_proj(x2, mix_mod, gain, wq[:, :D_MODEL].astype(BF16), wq[:, D_MODEL:].T.astype(BF16),
                                 eye_seg, jnp.tile(nsa_q_norm[bl], NSA_GROUP)[None, :], cosn, sinn, batch, seq)
            o_t = _attention(qn, qr, gt, *shared, ovl_t, batch, seq)
            x2 = _out_proj(x2, mix_mod, o_t, nsa_w_out[bl].astype(BF16), batch, seq)
        x2 = _ffn(x2, mods[2 * layer + 1], norm_ffn[layer][None, :], ffn_w_in[layer].astype(BF16),
                  ffn_w_out[layer].astype(BF16), seq)
    return x2.reshape(batch, seq, D_MODEL)
```

```python
import functools

import jax
import jax.numpy as jnp
from jax import lax
from jax.experimental import pallas as pl
from jax.experimental.pallas import tpu as pltpu

F32 = jnp.float32
BF16 = jnp.bfloat16

D_MODEL = 1024
DEPTH = 4
N_A_LAYERS = DEPTH // 2
HGRN_HEADS = 8
HGRN_DK = 128
NSA_HEADS = 16
NSA_KV_HEADS = 4
NSA_GROUP = 4
HEAD_DIM = 64
N_BRANCH = 3
CMP_BLOCK = 32
CMP_STRIDE = 16
CMP_HIDDEN = 256
SEL_BLOCK = 64
N_SELECT = 16
WINDOW = 512
ROPE_THETA = 10000.0
FFN_HIDDEN = 2816
EPS = 1e-6
NEG_INF = -1e30
FORCE_SCORE = 1e9

LANES = 128
MXU_N = 256
VMEM_LIMIT = 56 * 1024 * 1024

ROW_TILE = 512
HGRN_CHUNK = 128
Q_TILE = 128
SEL_KEYS = 256
SEL_BIG = 1024
FFN_CHUNK = 256
V_ROWS = HEAD_DIM + 16
LOG2E = 1.4426950408889634
CAUSAL_VARIANTS = 4


def _params(sem):
    return pltpu.CompilerParams(dimension_semantics=sem, vmem_limit_bytes=VMEM_LIMIT)


def _const_spec(shape):
    nd = len(shape)
    return pl.BlockSpec(shape, lambda *_: (0,) * nd, pipeline_mode=pl.Buffered(1))


def _dot(a, b):
    return jnp.dot(a, b, preferred_element_type=F32)


def _dot_nt(a, b):
    return lax.dot_general(a, b, (((1,), (1,)), ((), ())), preferred_element_type=F32)


def _dot_tn(a, b):
    return lax.dot_general(a, b, (((0,), (0,)), ((), ())), preferred_element_type=F32)


def _split(a):
    hi = a.astype(BF16)
    lo = (a - hi.astype(F32)).astype(BF16)
    return hi, lo


def _sigmoid(x):
    return 1.0 / (1.0 + jnp.exp(-x))


def _silu(x):
    return x * _sigmoid(x)


def _mod_norm(x, gain, shift, scale):
    ms = jnp.mean(x * x, axis=-1, keepdims=True)
    y = x * lax.rsqrt(ms + EPS) * gain
    return y * (1.0 + scale) + shift


def _mods_kernel(c_ref, w_ref, b_ref, o_ref):
    c = c_ref[...]
    ah, al = _split(_silu(c))
    wh, wl = _split(w_ref[0])
    o_ref[0] = _dot(ah, wh) + _dot(al, wh) + _dot(ah, wl) + b_ref[0]


def _mods(c_pad, w, b):
    n_l, _, n = w.shape
    tn = 1024
    return pl.pallas_call(
        _mods_kernel,
        grid=(n_l, n // tn),
        in_specs=[
            pl.BlockSpec((8, D_MODEL), lambda l, j: (0, 0)),
            pl.BlockSpec((1, D_MODEL, tn), lambda l, j: (l, 0, j)),
            pl.BlockSpec((1, 1, tn), lambda l, j: (l, 0, j)),
        ],
        out_specs=pl.BlockSpec((1, 8, tn), lambda l, j: (l, 0, j)),
        out_shape=jax.ShapeDtypeStruct((n_l, 8, n), F32),
        compiler_params=_params(("arbitrary", "arbitrary")),
        name="adaln_mods",
    )(c_pad, w, b.reshape(n_l, 1, n))


def _ffn_kernel(x_ref, mod_ref, gain_ref, win_ref, wout_ref, o_ref):
    x = x_ref[...]
    m = mod_ref[0]
    h = _mod_norm(x, gain_ref[...], m[0:1], m[1:2]).astype(BF16)
    acc = jnp.zeros(x.shape, F32)
    for c in range(FFN_HIDDEN // FFN_CHUNK):
        lo = c * FFN_CHUNK
        a = _dot(h, win_ref[:, lo:lo + FFN_CHUNK])
        b = _dot(h, win_ref[:, FFN_HIDDEN + lo:FFN_HIDDEN + lo + FFN_CHUNK])
        g = (_silu(a) * b).astype(BF16)
        acc = acc + _dot(g, wout_ref[lo:lo + FFN_CHUNK, :])
    o_ref[...] = x + m[2:3] * acc


def _ffn(x2, mod, gain, w_in, w_out, seq):
    t = x2.shape[0]
    per_b = seq // ROW_TILE
    return pl.pallas_call(
        _ffn_kernel,
        grid=(t // ROW_TILE,),
        in_specs=[
            pl.BlockSpec((ROW_TILE, D_MODEL), lambda i: (i, 0)),
            pl.BlockSpec((1, 3, D_MODEL), lambda i: (i // per_b, 0, 0)),
            _const_spec((1, D_MODEL)),
            _const_spec((D_MODEL, 2 * FFN_HIDDEN)),
            _const_spec((FFN_HIDDEN, D_MODEL)),
        ],
        out_specs=pl.BlockSpec((ROW_TILE, D_MODEL), lambda i: (i, 0)),
        out_shape=jax.ShapeDtypeStruct(x2.shape, F32),
        compiler_params=_params(("arbitrary",)),
        name="ffn",
    )(x2, mod, gain, w_in, w_out)


def _hgrn_kernel(x_ref, mod_ref, gain_ref, win_ref, lbraw_ref, onorm_ref, wout_ref,
                 o_ref, proj_scr, st_scr, oall_scr, *, layer):
    tc = HGRN_CHUNK
    dk = HGRN_DK

    @pl.when(pl.program_id(1) == 0)
    def _():
        st_scr[...] = jnp.zeros(st_scr.shape, F32)

    x = x_ref[...]
    m = mod_ref[0]
    h = _mod_norm(x, gain_ref[...], m[0:1], m[1:2]).astype(BF16)
    proj_scr[...] = _dot(h, win_ref[...])

    if layer > 0:
        raw = lbraw_ref[...]
        e = jnp.exp(raw - jnp.max(raw, axis=0, keepdims=True))
        sm = e / jnp.sum(e, axis=0, keepdims=True)
        lb = jnp.sum(sm[1:layer + 1], axis=0, keepdims=True)

    row = lax.broadcasted_iota(jnp.int32, (tc, D_MODEL), 0)
    r_i = lax.broadcasted_iota(jnp.int32, (tc, tc), 0)
    c_i = lax.broadcasted_iota(jnp.int32, (tc, tc), 1)
    ones_b = jnp.ones((dk, dk), BF16)
    n_lvl = tc.bit_length() - 1

    def chunk(ci, carry):
        r0 = pl.multiple_of(ci * tc, tc)
        qp = proj_scr[pl.ds(r0, tc), 0:D_MODEL]
        fp = proj_scr[pl.ds(r0, tc), D_MODEL:2 * D_MODEL]
        v = proj_scr[pl.ds(r0, tc), 2 * D_MODEL:3 * D_MODEL].astype(BF16)
        gp = proj_scr[pl.ds(r0, tc), 3 * D_MODEL:4 * D_MODEL]

        q = _silu(qp)
        e = jnp.exp(-jnp.abs(fp))
        r = 1.0 / (1.0 + e)
        pos = fp >= 0.0
        sig = jnp.where(pos, r, e * r)
        nsig = jnp.where(pos, e * r, r)
        if layer == 0:
            logf = jnp.minimum(fp, 0.0) - jnp.log(1.0 + e)
            kk = nsig
        else:
            logf = jnp.log(lb + (1.0 - lb) * sig)
            kk = (1.0 - lb) * nsig

        g_cum = logf
        s = 1
        while s < tc:
            g_cum = g_cum + jnp.where(row >= s, pltpu.roll(g_cum, s, 0), 0.0)
            s *= 2

        p_acc = [jnp.zeros((tc, tc), F32) for _ in range(HGRN_HEADS)]
        end_val = g_cum
        for lvl in range(n_lvl):
            hs = 1 << lvl
            second = (row & hs) != 0
            ref_val = jnp.where(second, pltpu.roll(end_val, hs, 0), end_val)
            decay = jnp.exp(-jnp.abs(g_cum - ref_val))
            qs = jnp.where(second, q * decay, 0.0).astype(BF16)
            ks = jnp.where(second, 0.0, kk * decay).astype(BF16)
            same = (r_i >> (lvl + 1)) == (c_i >> (lvl + 1))
            for hd in range(HGRN_HEADS):
                sl = slice(hd * dk, (hd + 1) * dk)
                p_acc[hd] = p_acc[hd] + jnp.where(same, _dot_nt(qs[:, sl], ks[:, sl]), 0.0)
            if lvl + 1 < n_lvl:
                end_val = jnp.where(second, end_val, pltpu.roll(end_val, tc - hs, 0))

        qk = (q * kk).astype(BF16)
        g_last = g_cum[tc - 1:tc, :]
        qe = (q * jnp.exp(g_cum)).astype(BF16)
        kd = (kk * jnp.exp(g_last - g_cum)).astype(BF16)
        s_decay = jnp.exp(g_last)
        gate_act = _silu(gp)
        onorm = onorm_ref[...]
        for hd in range(HGRN_HEADS):
            sl = slice(hd * dk, (hd + 1) * dk)
            p_h = p_acc[hd] + jnp.where(r_i == c_i, _dot(qk[:, sl], ones_b), 0.0)
            st = st_scr[hd]
            o_h = _dot(p_h.astype(BF16), v[:, sl]) + _dot_nt(qe[:, sl], st.astype(BF16))
            st_scr[hd] = s_decay[:, sl] * st + _dot_tn(v[:, sl], kd[:, sl])
            ms = jnp.mean(o_h * o_h, axis=-1, keepdims=True)
            o_n = o_h * lax.rsqrt(ms + EPS) * onorm
            oall_scr[pl.ds(r0, tc), sl] = (o_n * gate_act[:, sl]).astype(BF16)
        return carry

    lax.fori_loop(0, ROW_TILE // tc, chunk, 0)
    o_ref[...] = x + m[2:3] * _dot(oall_scr[...], wout_ref[...])


def _hgrn(x2, mod, gain, w_in, lb_raw, onorm, w_out, batch, seq, layer):
    per_b = seq // ROW_TILE
    return pl.pallas_call(
        functools.partial(_hgrn_kernel, layer=layer),
        grid=(batch, per_b),
        in_specs=[
            pl.BlockSpec((ROW_TILE, D_MODEL), lambda b, j: (b * per_b + j, 0)),
            pl.BlockSpec((1, 3, D_MODEL), lambda b, j: (b, 0, 0)),
            _const_spec((1, D_MODEL)),
            _const_spec((D_MODEL, 4 * D_MODEL)),
            _const_spec((N_A_LAYERS, D_MODEL)),
            _const_spec((1, HGRN_DK)),
            _const_spec((D_MODEL, D_MODEL)),
        ],
        out_specs=pl.BlockSpec((ROW_TILE, D_MODEL), lambda b, j: (b * per_b + j, 0)),
        out_shape=jax.ShapeDtypeStruct(x2.shape, F32),
        scratch_shapes=[
            pltpu.VMEM((ROW_TILE, 4 * D_MODEL), F32),
            pltpu.VMEM((HGRN_HEADS, HGRN_DK, HGRN_DK), F32),
            pltpu.VMEM((ROW_TILE, D_MODEL), BF16),
        ],
        compiler_params=_params(("arbitrary", "arbitrary")),
        name=f"hgrn{layer}",
    )(x2, mod, gain, w_in, lb_raw, onorm, w_out)


def _segnorm64(xc, seg_ones, gain):
    hi, lo = _split(xc * xc)
    ss = _dot(hi, seg_ones) + _dot(lo, seg_ones)
    return xc * lax.rsqrt(ss * (1.0 / HEAD_DIM) + EPS) * gain


def _rope_lanes(xp, cosn, sinn):
    lane = lax.broadcasted_iota(jnp.int32, xp.shape, 1)
    first = (lane & (HEAD_DIM // 2)) == 0
    rot = jnp.where(first, pltpu.roll(xp, LANES - HEAD_DIM // 2, 1), pltpu.roll(xp, HEAD_DIM // 2, 1))
    return xp * cosn + rot * sinn


def _rope_wide(xc, cosn, sinn):
    return jnp.concatenate(
        [_rope_lanes(xc[:, i * LANES:(i + 1) * LANES], cosn, sinn) for i in range(xc.shape[1] // LANES)], axis=1)


def _kv_kernel(x_ref, mod_ref, gain_ref, wn_ref, wvt_ref, seg_ref, gsel_ref, gwin_ref, cos_ref, sin_ref,
               kcmp_ref, vcmp_ref, ksel_ref, kwin_ref, vselt_ref, vwint_ref):
    x = x_ref[...]
    m = mod_ref[0]
    h = _mod_norm(x, gain_ref[...], m[0:1], m[1:2]).astype(BF16)
    nat = _dot(h, wn_ref[...])
    vt = _dot_nt(wvt_ref[...], h)
    w = NSA_KV_HEADS * HEAD_DIM
    seg = seg_ref[...]
    cosn = cos_ref[...]
    sinn = sin_ref[...]
    kcmp = nat[:, 0:w].astype(BF16)
    vcmp = nat[:, w:2 * w].astype(BF16)
    ksel = _rope_wide(_segnorm64(nat[:, 2 * w:3 * w], seg, gsel_ref[...]), cosn, sinn).astype(BF16)
    kwin = _rope_wide(_segnorm64(nat[:, 3 * w:4 * w], seg, gwin_ref[...]), cosn, sinn).astype(BF16)
    ones = jnp.ones((V_ROWS - HEAD_DIM, x.shape[0]), BF16)
    for hd in range(NSA_KV_HEADS):
        sl = slice(hd * HEAD_DIM, (hd + 1) * HEAD_DIM)
        kcmp_ref[0, hd] = kcmp[:, sl]
        vcmp_ref[0, hd] = vcmp[:, sl]
        ksel_ref[0, hd] = ksel[:, sl]
        kwin_ref[0, hd] = kwin[:, sl]
        vselt_ref[0, hd, 0:HEAD_DIM, :] = vt[hd * HEAD_DIM:(hd + 1) * HEAD_DIM, :].astype(BF16)
        vselt_ref[0, hd, HEAD_DIM:V_ROWS, :] = ones
        vwint_ref[0, hd, 0:HEAD_DIM, :] = vt[w + hd * HEAD_DIM:w + (hd + 1) * HEAD_DIM, :].astype(BF16)
        vwint_ref[0, hd, HEAD_DIM:V_ROWS, :] = ones


def _kv_prep(x2, mod, gain, w_nat, w_vt, seg, gsel, gwin, cosn, sinn, batch, seq):
    per_b = seq // ROW_TILE
    w = NSA_KV_HEADS * HEAD_DIM
    nat_spec = pl.BlockSpec((1, NSA_KV_HEADS, ROW_TILE, HEAD_DIM), lambda b, j: (b, 0, j, 0))
    tr_spec = pl.BlockSpec((1, NSA_KV_HEADS, V_ROWS, ROW_TILE), lambda b, j: (b, 0, 0, j))
    nat_shape = jax.ShapeDtypeStruct((batch, NSA_KV_HEADS, seq, HEAD_DIM), BF16)
    tr_shape = jax.ShapeDtypeStruct((batch, NSA_KV_HEADS, V_ROWS, seq), BF16)
    return pl.pallas_call(
        _kv_kernel,
        grid=(batch, per_b),
        in_specs=[
            pl.BlockSpec((ROW_TILE, D_MODEL), lambda b, j: (b * per_b + j, 0)),
            pl.BlockSpec((1, 2, D_MODEL), lambda b, j: (b, 0, 0)),
            _const_spec((1, D_MODEL)),
            _const_spec((D_MODEL, 4 * w)),
            _const_spec((2 * w, D_MODEL)),
            _const_spec((w, w)),
            _const_spec((1, w)),
            _const_spec((1, w)),
            pl.BlockSpec((ROW_TILE, LANES), lambda b, j: (j, 0)),
            pl.BlockSpec((ROW_TILE, LANES), lambda b, j: (j, 0)),
        ],
        out_specs=[nat_spec, nat_spec, nat_spec, nat_spec, tr_spec, tr_spec],
        out_shape=[nat_shape, nat_shape, nat_shape, nat_shape, tr_shape, tr_shape],
        compiler_params=_params(("arbitrary", "arbitrary")),
        name="nsa_kv",
    )(x2, mod, gain, w_nat, w_vt, seg, gsel, gwin, cosn, sinn)


def _compress_pre(r_ref, w1_ref, pos_ref):
    r = r_ref[0, 0]
    half = CMP_STRIDE * HEAD_DIM
    w1a = w1_ref[0:half, :]
    w1b = w1_ref[half:2 * half, :]
    n_rows = r.shape[0]
    u = _dot(r, w1a)
    v = _dot(r, w1b)
    pos = pos_ref[...]
    ph, pl_ = _split(pos)
    bias = (_dot(ph[:, 0:half], w1a) + _dot(pl_[:, 0:half], w1a)
            + _dot(ph[:, half:], w1b) + _dot(pl_[:, half:], w1b))[0:1, :]
    pre = u + pltpu.roll(v, n_rows - 1, 0) + bias
    return _silu(pre).astype(BF16)


def _compress_k_kernel(r_ref, w1_ref, pos_ref, w2_ref, gain_ref, o_ref):
    hid = _compress_pre(r_ref, w1_ref, pos_ref)
    out = _dot(hid, w2_ref[...])
    ms = jnp.mean(out * out, axis=-1, keepdims=True)
    o_ref[0, 0] = (out * lax.rsqrt(ms + EPS) * gain_ref[...]).astype(BF16)


def _compress_v_kernel(r_ref, w1_ref, pos_ref, w2t_ref, o_ref):
    hid = _compress_pre(r_ref, w1_ref, pos_ref)
    o_ref[0, 0, 0:HEAD_DIM, :] = _dot_nt(w2t_ref[...], hid).astype(BF16)
    o_ref[0, 0, HEAD_DIM:V_ROWS, :] = jnp.ones((V_ROWS - HEAD_DIM, hid.shape[0]), BF16)


def _compress(r, w1, pos8, w2, gain, batch, n_rows, transposed):
    half2 = CMP_BLOCK * HEAD_DIM
    in_specs = [
        pl.BlockSpec((1, 1, n_rows, CMP_STRIDE * HEAD_DIM), lambda b, hd: (b, hd, 0, 0)),
        _const_spec((half2, CMP_HIDDEN)),
        _const_spec((8, half2)),
    ]
    if transposed:
        kern = _compress_v_kernel
        in_specs.append(_const_spec((HEAD_DIM, CMP_HIDDEN)))
        args = (r, w1, pos8, w2)
        out_spec = pl.BlockSpec((1, 1, V_ROWS, n_rows), lambda b, hd: (b, hd, 0, 0))
        out_shape = jax.ShapeDtypeStruct((batch, NSA_KV_HEADS, V_ROWS, n_rows), BF16)
    else:
        kern = _compress_k_kernel
        in_specs += [_const_spec((CMP_HIDDEN, HEAD_DIM)), _const_spec((1, HEAD_DIM))]
        args = (r, w1, pos8, w2, gain)
        out_spec = pl.BlockSpec((1, 1, n_rows, HEAD_DIM), lambda b, hd: (b, hd, 0, 0))
        out_shape = jax.ShapeDtypeStruct((batch, NSA_KV_HEADS, n_rows, HEAD_DIM), BF16)
    return pl.pallas_call(
        kern,
        grid=(batch, NSA_KV_HEADS),
        in_specs=in_specs,
        out_specs=out_spec,
        out_shape=out_shape,
        compiler_params=_params(("arbitrary", "arbitrary")),
        name="nsa_compress_v" if transposed else "nsa_compress_k",
    )(*args)


def _q_kernel(x_ref, mod_ref, gain_ref, wq_ref, wgt_ref, seg_ref, qgain_ref, cos_ref, sin_ref,
              qn_ref, qr_ref, gt_ref):
    x = x_ref[...]
    m = mod_ref[0]
    h = _mod_norm(x, gain_ref[...], m[0:1], m[1:2]).astype(BF16)
    gt_ref[0] = _sigmoid(_dot_nt(wgt_ref[...], h))
    seg = seg_ref[...]
    cosn = cos_ref[...]
    sinn = sin_ref[...]
    scale = HEAD_DIM ** -0.5 * LOG2E
    w = NSA_GROUP * HEAD_DIM
    for c in range(NSA_KV_HEADS):
        qc = _dot(h, wq_ref[:, c * w:(c + 1) * w])
        qn = _segnorm64(qc, seg, qgain_ref[...]) * scale
        qr = _rope_wide(qn, cosn, sinn)
        qn = qn.astype(BF16)
        qr = qr.astype(BF16)
        for g in range(NSA_GROUP):
            sl = slice(g * HEAD_DIM, (g + 1) * HEAD_DIM)
            qn_ref[0, c * NSA_GROUP + g] = qn[:, sl]
            qr_ref[0, c * NSA_GROUP + g] = qr[:, sl]


def _q_proj(x2, mod, gain, w_q, w_gt, seg, qgain, cosn, sinn, batch, seq):
    per_b = seq // ROW_TILE
    w = NSA_GROUP * HEAD_DIM
    n_gate = N_BRANCH * NSA_HEADS
    q_spec = pl.BlockSpec((1, NSA_HEADS, ROW_TILE, HEAD_DIM), lambda b, j: (b, 0, j, 0))
    q_shape = jax.ShapeDtypeStruct((batch, NSA_HEADS, seq, HEAD_DIM), BF16)
    return pl.pallas_call(
        _q_kernel,
        grid=(batch, per_b),
        in_specs=[
            pl.BlockSpec((ROW_TILE, D_MODEL), lambda b, j: (b * per_b + j, 0)),
            pl.BlockSpec((1, 3, D_MODEL), lambda b, j: (b, 0, 0)),
            _const_spec((1, D_MODEL)),
            _const_spec((D_MODEL, D_MODEL)),
            _const_spec((n_gate, D_MODEL)),
            _const_spec((w, w)),
            _const_spec((1, w)),
            pl.BlockSpec((ROW_TILE, LANES), lambda b, j: (j, 0)),
            pl.BlockSpec((ROW_TILE, LANES), lambda b, j: (j, 0)),
        ],
        out_specs=[q_spec, q_spec, pl.BlockSpec((1, n_gate, ROW_TILE), lambda b, j: (b, 0, j))],
        out_shape=[q_shape, q_shape, jax.ShapeDtypeStruct((batch, n_gate, seq), F32)],
        compiler_params=_params(("arbitrary", "arbitrary")),
        name="nsa_q",
    )(x2, mod, gain, w_q, w_gt, seg, qgain, cosn, sinn)


def _attn_kernel(qn_ref, qr_ref, gt_ref, kc_ref, vct_ref, ks_ref, vst_ref, kw_ref, vwt_ref, ovl_ref,
                 o_ref, sel_scr, oc_scr, s_a, s_b, p_a, p_b, *, n_cmp, n_sel):
    qb = Q_TILE
    ng = NSA_GROUP
    hk = pl.program_id(1)
    qi = pl.program_id(2)
    t0 = qi * qb
    q_n = qn_ref[0].reshape(ng * qb, HEAD_DIM)
    q_r = qr_ref[0].reshape(ng * qb, HEAD_DIM)

    def tq(rows, cols):
        lane = lax.broadcasted_iota(jnp.int32, (rows, cols), 1)
        return t0 + (lane & (qb - 1))

    n_win = WINDOW + qb
    w0 = pl.multiple_of(jnp.maximum(t0 - WINDOW, 0), qb)
    s_w = _dot_nt(kw_ref[0, 0, pl.ds(w0, n_win), :], q_r)
    kpos = w0 + lax.broadcasted_iota(jnp.int32, (n_win, qb), 0)
    t_w = tq(n_win, qb)
    bias_w = jnp.where((kpos <= t_w) & (kpos > t_w - WINDOW), 0.0, NEG_INF)
    sb_w = s_w + jnp.concatenate([bias_w] * ng, axis=1)
    p_w = jnp.exp2(sb_w - jnp.max(sb_w, axis=0, keepdims=True)).astype(BF16)
    acc_w = _dot(vwt_ref[0, 0, :, pl.ds(w0, n_win)], p_w)

    def compress_and_select(n_c, n_s):
        c_end = lax.broadcasted_iota(jnp.int32, (n_c, qb), 0) * CMP_STRIDE + (CMP_BLOCK - 1)
        bias_c = jnp.where(c_end <= tq(n_c, qb), 0.0, NEG_INF)
        sb_c = _dot_nt(kc_ref[0, 0, 0:n_c, :], q_n) + jnp.concatenate([bias_c] * ng, axis=1)
        m_c = jnp.max(sb_c, axis=0, keepdims=True)
        e_c = jnp.exp2(sb_c - m_c).astype(BF16)
        acc_c = _dot(vct_ref[0, 0, :, 0:n_c], e_c)
        inv_c = jnp.where(m_c > 0.5 * NEG_INF, 1.0 / acc_c[HEAD_DIM:HEAD_DIM + 1], 0.0)
        oc_scr[...] = acc_c[0:HEAD_DIM] * inv_c

        imp_g = _dot(ovl_ref[0:n_s, 0:n_c], e_c) * inv_c
        imp = imp_g[:, 0:qb]
        for g in range(1, ng):
            imp = imp + imp_g[:, g * qb:(g + 1) * qb]
        j_i = lax.broadcasted_iota(jnp.int32, (n_s, qb), 0)
        cur = (t0 + lax.broadcasted_iota(jnp.int32, (n_s, qb), 1)) // SEL_BLOCK
        valid = j_i <= cur
        forced = (j_i == 0) | (j_i == cur) | (j_i == cur - 1)
        val = jnp.where(valid & jnp.logical_not(forced), imp, -1.0)
        for _ in range(N_SELECT - 3):
            mx = jnp.max(val, axis=0, keepdims=True)
            idx = jnp.min(jnp.where(val == mx, j_i, n_s), axis=0, keepdims=True)
            val = jnp.where((j_i == idx) & (mx >= 0.0), -3.0e38, val)
        chosen = (forced & valid) | (val < -1.0e38)
        sel_scr[0:n_s, :] = jnp.where(chosen & (j_i < t0 // SEL_BLOCK), 0.0, NEG_INF)
        if n_s < n_sel:
            sel_scr[n_s:n_sel, :] = jnp.full((n_sel - n_s, qb), NEG_INF, F32)

    variant = (t0 + qb - 1) // (n_sel * SEL_BLOCK // CAUSAL_VARIANTS)
    for v in range(CAUSAL_VARIANTS):
        pl.when(variant == v)(functools.partial(
            compress_and_select, (v + 1) * n_cmp // CAUSAL_VARIANTS, (v + 1) * n_sel // CAUSAL_VARIANTS))
    o_c = oc_scr[...]

    n_blk = SEL_BIG // SEL_BLOCK
    last_tile = n_sel // n_blk - 1

    def qk(kt, s_ref):
        k0 = pl.multiple_of(kt * SEL_BIG, SEL_BIG)
        rows = sel_scr[pl.ds(kt * n_blk, n_blk), :]
        bias = jnp.concatenate(
            [jnp.broadcast_to(rows[i:i + 1, :], (SEL_BLOCK, qb)) for i in range(n_blk)], axis=0)
        sb = _dot_nt(ks_ref[0, 0, pl.ds(k0, SEL_BIG), :], q_r) + jnp.concatenate([bias] * ng, axis=1)
        s_ref[...] = sb
        return jnp.max(sb, axis=0, keepdims=True)

    def soft(s_ref, p_ref, m_old, tile_max):
        m_new = jnp.maximum(m_old, tile_max)
        p_ref[...] = jnp.exp2(s_ref[...] - m_new).astype(BF16)
        return m_new, jnp.exp2(m_old - m_new)

    def pv(kt, p_ref, alpha, acc):
        k0 = pl.multiple_of(kt * SEL_BIG, SEL_BIG)
        return alpha * acc + _dot(vst_ref[0, 0, :, pl.ds(k0, SEL_BIG)], p_ref[...])

    def pair(j, carry):
        m, alpha_prev, acc, max_a = carry
        ta = 2 * j
        max_b = qk(ta + 1, s_b)
        m, alpha_a = soft(s_a, p_a, m, max_a)
        acc = pv(jnp.maximum(ta - 1, 0), p_b, alpha_prev, acc)
        max_a = qk(jnp.minimum(ta + 2, last_tile), s_a)
        m, alpha_b = soft(s_b, p_b, m, max_b)
        acc = pv(ta, p_a, alpha_a, acc)
        return m, alpha_b, acc, max_a

    n_tiles = (t0 + SEL_BIG - 1) // SEL_BIG
    n_pairs = n_tiles // 2
    p_b[...] = jnp.zeros(p_b.shape, BF16)
    m_s, alpha_s, acc_s, max_a = lax.fori_loop(
        0, n_pairs, pair,
        (jnp.full((1, ng * qb), NEG_INF, F32), jnp.ones((1, ng * qb), F32), jnp.zeros((V_ROWS, ng * qb), F32),
         qk(0, s_a)))
    acc_s = pv(jnp.maximum(2 * n_pairs - 1, 0), p_b, alpha_s, acc_s)

    def odd_tile(args):
        m, acc = args
        m, alpha = soft(s_a, p_a, m, max_a)
        return m, pv(n_tiles - 1, p_a, alpha, acc)

    m_s, acc_s = lax.cond(n_tiles % 2 == 1, odd_tile, lambda args: args, (m_s, acc_s))

    d0 = pl.multiple_of(t0, qb)
    r_d = lax.broadcasted_iota(jnp.int32, (qb, qb), 0)
    c_d = lax.broadcasted_iota(jnp.int32, (qb, qb), 1)
    sb_d = (_dot_nt(ks_ref[0, 0, pl.ds(d0, qb), :], q_r)
            + jnp.concatenate([jnp.where(r_d <= c_d, 0.0, NEG_INF)] * ng, axis=1))
    m_d = jnp.maximum(m_s, jnp.max(sb_d, axis=0, keepdims=True))
    acc_s = jnp.exp2(m_s - m_d) * acc_s + _dot(vst_ref[0, 0, :, pl.ds(d0, qb)], jnp.exp2(sb_d - m_d).astype(BF16))

    o_s = acc_s[0:HEAD_DIM] * (1.0 / acc_s[HEAD_DIM:HEAD_DIM + 1])
    o_w = acc_w[0:HEAD_DIM] * (1.0 / acc_w[HEAD_DIM:HEAD_DIM + 1])
    g_c = gt_ref[0, pl.ds(hk * ng, ng), :]
    g_s = gt_ref[0, pl.ds(NSA_HEADS + hk * ng, ng), :]
    g_w = gt_ref[0, pl.ds(2 * NSA_HEADS + hk * ng, ng), :]
    for g in range(ng):
        sl = slice(g * qb, (g + 1) * qb)
        mix = g_c[g:g + 1, :] * o_c[:, sl] + g_s[g:g + 1, :] * o_s[:, sl] + g_w[g:g + 1, :] * o_w[:, sl]
        o_ref[0, g * HEAD_DIM:(g + 1) * HEAD_DIM, :] = mix.astype(BF16)


def _attention(qn, qr, gt, kc, vct, ks, vst, kw, vwt, ovl_t, batch, seq):
    assert seq % (2 * SEL_BIG) == 0 and seq >= WINDOW + Q_TILE and seq // SEL_BLOCK >= N_SELECT * CAUSAL_VARIANTS
    n_cmp = seq // CMP_STRIDE
    n_sel = seq // SEL_BLOCK
    qb = Q_TILE
    n_gate = N_BRANCH * NSA_HEADS
    q_spec = pl.BlockSpec((1, NSA_GROUP, qb, HEAD_DIM), lambda b, hk, i: (b, hk, i, 0))
    nat_full = pl.BlockSpec((1, 1, seq, HEAD_DIM), lambda b, hk, i: (b, hk, 0, 0))
    tr_full = pl.BlockSpec((1, 1, V_ROWS, seq), lambda b, hk, i: (b, hk, 0, 0))
    return pl.pallas_call(
        functools.partial(_attn_kernel, n_cmp=n_cmp, n_sel=n_sel),
        grid=(batch, NSA_KV_HEADS, seq // qb),
        in_specs=[
            q_spec, q_spec,
            pl.BlockSpec((1, n_gate, qb), lambda b, hk, i: (b, 0, i)),
            pl.BlockSpec((1, 1, n_cmp, HEAD_DIM), lambda b, hk, i: (b, hk, 0, 0)),
            pl.BlockSpec((1, 1, V_ROWS, n_cmp), lambda b, hk, i: (b, hk, 0, 0)),
            nat_full, tr_full, nat_full, tr_full,
            _const_spec((n_sel, n_cmp)),
        ],
        out_specs=pl.BlockSpec((1, NSA_GROUP * HEAD_DIM, qb), lambda b, hk, i: (b, hk, i)),
        out_shape=jax.ShapeDtypeStruct((batch, D_MODEL, seq), BF16),
        scratch_shapes=[
            pltpu.VMEM((n_sel, qb), F32),
            pltpu.VMEM((HEAD_DIM, NSA_GROUP * qb), F32),
            pltpu.VMEM((SEL_BIG, NSA_GROUP * qb), F32),
            pltpu.VMEM((SEL_BIG, NSA_GROUP * qb), F32),
            pltpu.VMEM((SEL_BIG, NSA_GROUP * qb), BF16),
            pltpu.VMEM((SEL_BIG, NSA_GROUP * qb), BF16),
        ],
        compiler_params=_params(("arbitrary", "arbitrary", "arbitrary")),
        name="nsa_attention",
    )(qn, qr, gt, kc, vct, ks, vst, kw, vwt, ovl_t)


def _out_kernel(x_ref, mod_ref, ot_ref, w_ref, o_ref):
    y = _dot_tn(ot_ref[0], w_ref[...])
    o_ref[...] = x_ref[...] + mod_ref[0][2:3] * y


def _out_proj(x2, mod, o_t, w_out, batch, seq):
    per_b = seq // ROW_TILE
    return pl.pallas_call(
        _out_kernel,
        grid=(batch, per_b),
        in_specs=[
            pl.BlockSpec((ROW_TILE, D_MODEL), lambda b, j: (b * per_b + j, 0)),
            pl.BlockSpec((1, 3, D_MODEL), lambda b, j: (b, 0, 0)),
            pl.BlockSpec((1, D_MODEL, ROW_TILE), lambda b, j: (b, 0, j)),
            _const_spec((D_MODEL, D_MODEL)),
        ],
        out_specs=pl.BlockSpec((ROW_TILE, D_MODEL), lambda b, j: (b * per_b + j, 0)),
        out_shape=jax.ShapeDtypeStruct(x2.shape, F32),
        compiler_params=_params(("arbitrary", "arbitrary")),
        name="nsa_out",
    )(x2, mod, o_t, w_out)


def _rope_lane_tables(seq):
    inv_freq = 1.0 / (ROPE_THETA ** (jnp.arange(0, HEAD_DIM, 2, dtype=F32) / HEAD_DIM))
    ang = jnp.arange(seq, dtype=F32)[:, None] * inv_freq[None, :]
    cos, sin = jnp.cos(ang), jnp.sin(ang)
    return jnp.tile(cos, (1, 4)), jnp.concatenate([-sin, sin, -sin, sin], axis=-1)


def _overlap_t(n_sel, n_cmp):
    cs = jnp.arange(n_cmp)[None, :] * CMP_STRIDE
    ss = jnp.arange(n_sel)[:, None] * SEL_BLOCK
    return ((cs < ss + SEL_BLOCK) & (cs + CMP_BLOCK > ss)).astype(BF16)


def kernel(x, c, ada_w, ada_b, norm_mix, norm_ffn, hgrn_w_in, hgrn_lower_bounds, hgrn_out_norm, hgrn_w_out,
           kv_ada_w, kv_ada_b, kv_norm, nsa_w_kv, nsa_k_norm, cmp_pos, cmp_w1, cmp_w2, nsa_w_q, nsa_q_norm,
           nsa_w_out, ffn_w_in, ffn_w_out):
    batch, seq, _ = x.shape
    w = NSA_KV_HEADS * HEAD_DIM
    x2 = x.reshape(batch * seq, D_MODEL)

    c_pad = jnp.zeros((8, D_MODEL), F32).at[:batch].set(c)
    mods = _mods(c_pad, ada_w, ada_b)[:, :batch].reshape(2 * DEPTH, batch, 3, D_MODEL)
    kv_mod = _mods(c_pad, kv_ada_w[None], kv_ada_b[None])[0, :batch].reshape(batch, 2, D_MODEL)

    cosn, sinn = _rope_lane_tables(seq)
    eye_seg = jnp.kron(jnp.eye(w // HEAD_DIM, dtype=F32), jnp.ones((HEAD_DIM, HEAD_DIM), F32)).astype(BF16)
    n_rows = seq // CMP_STRIDE
    ovl_t = _overlap_t(seq // SEL_BLOCK, n_rows)
    shared = None

    for layer in range(DEPTH):
        mix_mod = mods[2 * layer]
        gain = norm_mix[layer][None, :]
        if layer < N_A_LAYERS:
            x2 = _hgrn(x2, mix_mod, gain, hgrn_w_in[layer].astype(BF16), hgrn_lower_bounds,
                       hgrn_out_norm[layer][None, :], hgrn_w_out[layer].astype(BF16), batch, seq, layer)
        else:
            if shared is None:
                wkv = nsa_w_kv.reshape(D_MODEL, 2 * N_BRANCH, w)
                w_nat = jnp.concatenate([wkv[:, 0], wkv[:, 1], wkv[:, 2], wkv[:, 4]], axis=1).astype(BF16)
                w_vt = jnp.concatenate([wkv[:, 3], wkv[:, 5]], axis=1).T.astype(BF16)
                kcmp, vcmp, ks, kw, vst, vwt = _kv_prep(
                    x2, kv_mod, kv_norm[None, :], w_nat, w_vt, eye_seg,
                    jnp.tile(nsa_k_norm[1], NSA_KV_HEADS)[None, :], jnp.tile(nsa_k_norm[2], NSA_KV_HEADS)[None, :],
                    cosn, sinn, batch, seq)
                pos8 = jnp.zeros((2, 8, CMP_BLOCK * HEAD_DIM), F32).at[:, 0].set(
                    cmp_pos.reshape(2, CMP_BLOCK * HEAD_DIM))
                rk = kcmp.reshape(batch, NSA_KV_HEADS, n_rows, CMP_STRIDE * HEAD_DIM)
                rv = vcmp.reshape(batch, NSA_KV_HEADS, n_rows, CMP_STRIDE * HEAD_DIM)
                kc = _compress(rk, cmp_w1[0].astype(BF16), pos8[0], cmp_w2[0].astype(BF16),
                               nsa_k_norm[0][None, :], batch, n_rows, False)
                vct = _compress(rv, cmp_w1[1].astype(BF16), pos8[1], cmp_w2[1].T.astype(BF16),
                                None, batch, n_rows, True)
                shared = (kc, vct, ks, vst, kw, vwt)
            bl = layer - N_A_LAYERS
            wq = nsa_w_q[bl]
            qn, qr, gt = _q_proj(x2, mix_mod, gain, wq[:, :D_MODEL].astype(BF16), wq[:, D_MODEL:].T.astype(BF16),
                                 eye_seg, jnp.tile(nsa_q_norm[bl], NSA_GROUP)[None, :], cosn, sinn, batch, seq)
            o_t = _attention(qn, qr, gt, *shared, ovl_t, batch, seq)
            x2 = _out_proj(x2, mix_mod, o_t, nsa_w_out[bl].astype(BF16), batch, seq)
        x2 = _ffn(x2, mods[2 * layer + 1], norm_ffn[layer][None, :], ffn_w_in[layer].astype(BF16),
                  ffn_w_out[layer].astype(BF16), seq)
    return x2.reshape(batch, seq, D_MODEL)
```

```python
import functools

import jax
import jax.numpy as jnp
from jax import lax
from jax.experimental import pallas as pl
from jax.experimental.pallas import tpu as pltpu

F32 = jnp.float32
BF16 = jnp.bfloat16

D_MODEL = 1024
DEPTH = 4
N_A_LAYERS = DEPTH // 2
HGRN_HEADS = 8
HGRN_DK = 128
NSA_HEADS = 16
NSA_KV_HEADS = 4
NSA_GROUP = 4
HEAD_DIM = 64
N_BRANCH = 3
CMP_BLOCK = 32
CMP_STRIDE = 16
CMP_HIDDEN = 256
SEL_BLOCK = 64
N_SELECT = 16
WINDOW = 512
ROPE_THETA = 10000.0
FFN_HIDDEN = 2816
EPS = 1e-6
NEG_INF = -1e30
FORCE_SCORE = 1e9

LANES = 128
MXU_N = 256
VMEM_LIMIT = 56 * 1024 * 1024

ROW_TILE = 512
HGRN_CHUNK = 128
Q_TILE = 256
SEL_KEYS = 256
SEL_BIG = 1024
FFN_CHUNK = 256
V_ROWS = HEAD_DIM + 16
LOG2E = 1.4426950408889634
CAUSAL_VARIANTS = 4


def _params(sem):
    return pltpu.CompilerParams(dimension_semantics=sem, vmem_limit_bytes=VMEM_LIMIT)


def _const_spec(shape):
    nd = len(shape)
    return pl.BlockSpec(shape, lambda *_: (0,) * nd, pipeline_mode=pl.Buffered(1))


def _dot(a, b):
    return jnp.dot(a, b, preferred_element_type=F32)


def _dot_nt(a, b):
    return lax.dot_general(a, b, (((1,), (1,)), ((), ())), preferred_element_type=F32)


def _dot_tn(a, b):
    return lax.dot_general(a, b, (((0,), (0,)), ((), ())), preferred_element_type=F32)


def _split(a):
    hi = a.astype(BF16)
    lo = (a - hi.astype(F32)).astype(BF16)
    return hi, lo


def _sigmoid(x):
    return 1.0 / (1.0 + jnp.exp(-x))


def _silu(x):
    return x * _sigmoid(x)


def _mod_norm(x, gain, shift, scale):
    ms = jnp.mean(x * x, axis=-1, keepdims=True)
    y = x * lax.rsqrt(ms + EPS) * gain
    return y * (1.0 + scale) + shift


def _mods_kernel(c_ref, w_ref, b_ref, o_ref):
    c = c_ref[...]
    ah, al = _split(_silu(c))
    wh, wl = _split(w_ref[0])
    o_ref[0] = _dot(ah, wh) + _dot(al, wh) + _dot(ah, wl) + b_ref[0]


def _mods(c_pad, w, b):
    n_l, _, n = w.shape
    tn = 1024
    return pl.pallas_call(
        _mods_kernel,
        grid=(n_l, n // tn),
        in_specs=[
            pl.BlockSpec((8, D_MODEL), lambda l, j: (0, 0)),
            pl.BlockSpec((1, D_MODEL, tn), lambda l, j: (l, 0, j)),
            pl.BlockSpec((1, 1, tn), lambda l, j: (l, 0, j)),
        ],
        out_specs=pl.BlockSpec((1, 8, tn), lambda l, j: (l, 0, j)),
        out_shape=jax.ShapeDtypeStruct((n_l, 8, n), F32),
        compiler_params=_params(("arbitrary", "arbitrary")),
        name="adaln_mods",
    )(c_pad, w, b.reshape(n_l, 1, n))


def _ffn_body(x, m, gain, win_ref, wout_ref):
    h = _mod_norm(x, gain, m[0:1], m[1:2]).astype(BF16)
    acc = jnp.zeros(x.shape, F32)
    for c in range(FFN_HIDDEN // FFN_CHUNK):
        lo = c * FFN_CHUNK
        a = _dot(h, win_ref[:, lo:lo + FFN_CHUNK])
        b = _dot(h, win_ref[:, FFN_HIDDEN + lo:FFN_HIDDEN + lo + FFN_CHUNK])
        g = (_silu(a) * b).astype(BF16)
        acc = acc + _dot(g, wout_ref[lo:lo + FFN_CHUNK, :])
    return x + m[2:3] * acc


def _ffn_kernel(x_ref, mod_ref, gain_ref, win_ref, wout_ref, o_ref):
    o_ref[...] = _ffn_body(x_ref[...], mod_ref[0], gain_ref[...], win_ref, wout_ref)


def _out_ffn_kernel(x_ref, mixmod_ref, ot_ref, wo_ref, mod_ref, gain_ref, win_ref, wout_ref, o_ref):
    x = x_ref[...] + mixmod_ref[0][2:3] * _dot_tn(ot_ref[0], wo_ref[...])
    o_ref[...] = _ffn_body(x, mod_ref[0], gain_ref[...], win_ref, wout_ref)


def _out_ffn(x2, mix_mod, o_t, w_o, mod, gain, w_in, w_out, batch, seq):
    per_b = seq // ROW_TILE
    row_spec = pl.BlockSpec((ROW_TILE, D_MODEL), lambda b, j: (b * per_b + j, 0))
    mod_spec = pl.BlockSpec((1, 3, D_MODEL), lambda b, j: (b, 0, 0))
    return pl.pallas_call(
        _out_ffn_kernel,
        grid=(batch, per_b),
        in_specs=[
            row_spec, mod_spec,
            pl.BlockSpec((1, D_MODEL, ROW_TILE), lambda b, j: (b, 0, j)),
            _const_spec((D_MODEL, D_MODEL)),
            mod_spec,
            _const_spec((1, D_MODEL)),
            _const_spec((D_MODEL, 2 * FFN_HIDDEN)),
            _const_spec((FFN_HIDDEN, D_MODEL)),
        ],
        out_specs=row_spec,
        out_shape=jax.ShapeDtypeStruct(x2.shape, F32),
        compiler_params=_params(("arbitrary", "arbitrary")),
        name="nsa_out_ffn",
    )(x2, mix_mod, o_t, w_o, mod, gain, w_in, w_out)


def _ffn(x2, mod, gain, w_in, w_out, seq):
    t = x2.shape[0]
    per_b = seq // ROW_TILE
    return pl.pallas_call(
        _ffn_kernel,
        grid=(t // ROW_TILE,),
        in_specs=[
            pl.BlockSpec((ROW_TILE, D_MODEL), lambda i: (i, 0)),
            pl.BlockSpec((1, 3, D_MODEL), lambda i: (i // per_b, 0, 0)),
            _const_spec((1, D_MODEL)),
            _const_spec((D_MODEL, 2 * FFN_HIDDEN)),
            _const_spec((FFN_HIDDEN, D_MODEL)),
        ],
        out_specs=pl.BlockSpec((ROW_TILE, D_MODEL), lambda i: (i, 0)),
        out_shape=jax.ShapeDtypeStruct(x2.shape, F32),
        compiler_params=_params(("arbitrary",)),
        name="ffn",
    )(x2, mod, gain, w_in, w_out)


def _hgrn_kernel(x_ref, mod_ref, gain_ref, win_ref, lbraw_ref, onorm_ref, wout_ref,
                 o_ref, proj_scr, st_scr, oall_scr, *, layer):
    tc = HGRN_CHUNK
    dk = HGRN_DK

    @pl.when(pl.program_id(1) == 0)
    def _():
        st_scr[...] = jnp.zeros(st_scr.shape, F32)

    x = x_ref[...]
    m = mod_ref[0]
    h = _mod_norm(x, gain_ref[...], m[0:1], m[1:2]).astype(BF16)
    proj_scr[...] = _dot(h, win_ref[...])

    if layer > 0:
        raw = lbraw_ref[...]
        e = jnp.exp(raw - jnp.max(raw, axis=0, keepdims=True))
        sm = e / jnp.sum(e, axis=0, keepdims=True)
        lb = jnp.sum(sm[1:layer + 1], axis=0, keepdims=True)

    row = lax.broadcasted_iota(jnp.int32, (tc, D_MODEL), 0)
    r_i = lax.broadcasted_iota(jnp.int32, (tc, tc), 0)
    c_i = lax.broadcasted_iota(jnp.int32, (tc, tc), 1)
    ones_b = jnp.ones((dk, dk), BF16)
    n_lvl = tc.bit_length() - 1

    def chunk(ci, carry):
        r0 = pl.multiple_of(ci * tc, tc)
        qp = proj_scr[pl.ds(r0, tc), 0:D_MODEL]
        fp = proj_scr[pl.ds(r0, tc), D_MODEL:2 * D_MODEL]
        v = proj_scr[pl.ds(r0, tc), 2 * D_MODEL:3 * D_MODEL].astype(BF16)
        gp = proj_scr[pl.ds(r0, tc), 3 * D_MODEL:4 * D_MODEL]

        q = _silu(qp)
        e = jnp.exp(-jnp.abs(fp))
        r = 1.0 / (1.0 + e)
        pos = fp >= 0.0
        sig = jnp.where(pos, r, e * r)
        nsig = jnp.where(pos, e * r, r)
        if layer == 0:
            logf = jnp.minimum(fp, 0.0) - jnp.log(1.0 + e)
            kk = nsig
        else:
            logf = jnp.log(lb + (1.0 - lb) * sig)
            kk = (1.0 - lb) * nsig

        g_cum = logf
        s = 1
        while s < tc:
            g_cum = g_cum + jnp.where(row >= s, pltpu.roll(g_cum, s, 0), 0.0)
            s *= 2

        p_acc = [jnp.zeros((tc, tc), F32) for _ in range(HGRN_HEADS)]
        end_val = g_cum
        for lvl in range(n_lvl):
            hs = 1 << lvl
            second = (row & hs) != 0
            ref_val = jnp.where(second, pltpu.roll(end_val, hs, 0), end_val)
            decay = jnp.exp(-jnp.abs(g_cum - ref_val))
            qs = jnp.where(second, q * decay, 0.0).astype(BF16)
            ks = jnp.where(second, 0.0, kk * decay).astype(BF16)
            same = (r_i >> (lvl + 1)) == (c_i >> (lvl + 1))
            for hd in range(HGRN_HEADS):
                sl = slice(hd * dk, (hd + 1) * dk)
                p_acc[hd] = p_acc[hd] + jnp.where(same, _dot_nt(qs[:, sl], ks[:, sl]), 0.0)
            if lvl + 1 < n_lvl:
                end_val = jnp.where(second, end_val, pltpu.roll(end_val, tc - hs, 0))

        qk = (q * kk).astype(BF16)
        g_last = g_cum[tc - 1:tc, :]
        qe = (q * jnp.exp(g_cum)).astype(BF16)
        kd = (kk * jnp.exp(g_last - g_cum)).astype(BF16)
        s_decay = jnp.exp(g_last)
        gate_act = _silu(gp)
        onorm = onorm_ref[...]
        for hd in range(HGRN_HEADS):
            sl = slice(hd * dk, (hd + 1) * dk)
            p_h = p_acc[hd] + jnp.where(r_i == c_i, _dot(qk[:, sl], ones_b), 0.0)
            st = st_scr[hd]
            o_h = _dot(p_h.astype(BF16), v[:, sl]) + _dot_nt(qe[:, sl], st.astype(BF16))
            st_scr[hd] = s_decay[:, sl] * st + _dot_tn(v[:, sl], kd[:, sl])
            ms = jnp.mean(o_h * o_h, axis=-1, keepdims=True)
            o_n = o_h * lax.rsqrt(ms + EPS) * onorm
            oall_scr[pl.ds(r0, tc), sl] = (o_n * gate_act[:, sl]).astype(BF16)
        return carry

    lax.fori_loop(0, ROW_TILE // tc, chunk, 0)
    o_ref[...] = x + m[2:3] * _dot(oall_scr[...], wout_ref[...])


def _hgrn(x2, mod, gain, w_in, lb_raw, onorm, w_out, batch, seq, layer):
    per_b = seq // ROW_TILE
    return pl.pallas_call(
        functools.partial(_hgrn_kernel, layer=layer),
        grid=(batch, per_b),
        in_specs=[
            pl.BlockSpec((ROW_TILE, D_MODEL), lambda b, j: (b * per_b + j, 0)),
            pl.BlockSpec((1, 3, D_MODEL), lambda b, j: (b, 0, 0)),
            _const_spec((1, D_MODEL)),
            _const_spec((D_MODEL, 4 * D_MODEL)),
            _const_spec((N_A_LAYERS, D_MODEL)),
            _const_spec((1, HGRN_DK)),
            _const_spec((D_MODEL, D_MODEL)),
        ],
        out_specs=pl.BlockSpec((ROW_TILE, D_MODEL), lambda b, j: (b * per_b + j, 0)),
        out_shape=jax.ShapeDtypeStruct(x2.shape, F32),
        scratch_shapes=[
            pltpu.VMEM((ROW_TILE, 4 * D_MODEL), F32),
            pltpu.VMEM((HGRN_HEADS, HGRN_DK, HGRN_DK), F32),
            pltpu.VMEM((ROW_TILE, D_MODEL), BF16),
        ],
        compiler_params=_params(("arbitrary", "arbitrary")),
        name=f"hgrn{layer}",
    )(x2, mod, gain, w_in, lb_raw, onorm, w_out)


def _segnorm64(xc, seg_ones, gain):
    hi, lo = _split(xc * xc)
    ss = _dot(hi, seg_ones) + _dot(lo, seg_ones)
    return xc * lax.rsqrt(ss * (1.0 / HEAD_DIM) + EPS) * gain


def _load_rows(load8, start, n):
    blk = load8(pl.multiple_of(start // 8 * 8, 8))
    off = start % 8
    out = blk[0:n]
    for o in range(1, 8 // n):
        out = jnp.where(off == o * n, blk[o * n:(o + 1) * n], out)
    return out


def _rope_lanes(xp, cosn, sinn):
    lane = lax.broadcasted_iota(jnp.int32, xp.shape, 1)
    first = (lane & (HEAD_DIM // 2)) == 0
    rot = jnp.where(first, pltpu.roll(xp, LANES - HEAD_DIM // 2, 1), pltpu.roll(xp, HEAD_DIM // 2, 1))
    return xp * cosn + rot * sinn


def _rope_wide(xc, cosn, sinn):
    return jnp.concatenate(
        [_rope_lanes(xc[:, i * LANES:(i + 1) * LANES], cosn, sinn) for i in range(xc.shape[1] // LANES)], axis=1)


def _kv_kernel(x_ref, mod_ref, gain_ref, wn_ref, wvt_ref, seg_ref, gsel_ref, gwin_ref, cos_ref, sin_ref,
               kcmp_ref, vcmp_ref, ksel_ref, kwin_ref, vselt_ref, vwint_ref):
    x = x_ref[...]
    m = mod_ref[0]
    h = _mod_norm(x, gain_ref[...], m[0:1], m[1:2]).astype(BF16)
    nat = _dot(h, wn_ref[...])
    vt = _dot_nt(wvt_ref[...], h)
    w = NSA_KV_HEADS * HEAD_DIM
    seg = seg_ref[...]
    cosn = cos_ref[...]
    sinn = sin_ref[...]
    kcmp = nat[:, 0:w].astype(BF16)
    vcmp = nat[:, w:2 * w].astype(BF16)
    ksel = _rope_wide(_segnorm64(nat[:, 2 * w:3 * w], seg, gsel_ref[...]), cosn, sinn).astype(BF16)
    kwin = _rope_wide(_segnorm64(nat[:, 3 * w:4 * w], seg, gwin_ref[...]), cosn, sinn).astype(BF16)
    ones = jnp.ones((V_ROWS - HEAD_DIM, x.shape[0]), BF16)
    for hd in range(NSA_KV_HEADS):
        sl = slice(hd * HEAD_DIM, (hd + 1) * HEAD_DIM)
        kcmp_ref[0, hd] = kcmp[:, sl]
        vcmp_ref[0, hd] = vcmp[:, sl]
        ksel_ref[0, hd] = ksel[:, sl]
        kwin_ref[0, hd] = kwin[:, sl]
        vselt_ref[0, hd, 0:HEAD_DIM, :] = vt[hd * HEAD_DIM:(hd + 1) * HEAD_DIM, :].astype(BF16)
        vselt_ref[0, hd, HEAD_DIM:V_ROWS, :] = ones
        vwint_ref[0, hd, 0:HEAD_DIM, :] = vt[w + hd * HEAD_DIM:w + (hd + 1) * HEAD_DIM, :].astype(BF16)
        vwint_ref[0, hd, HEAD_DIM:V_ROWS, :] = ones


def _kv_prep(x2, mod, gain, w_nat, w_vt, seg, gsel, gwin, cosn, sinn, batch, seq):
    per_b = seq // ROW_TILE
    w = NSA_KV_HEADS * HEAD_DIM
    nat_spec = pl.BlockSpec((1, NSA_KV_HEADS, ROW_TILE, HEAD_DIM), lambda b, j: (b, 0, j, 0))
    tr_spec = pl.BlockSpec((1, NSA_KV_HEADS, V_ROWS, ROW_TILE), lambda b, j: (b, 0, 0, j))
    nat_shape = jax.ShapeDtypeStruct((batch, NSA_KV_HEADS, seq, HEAD_DIM), BF16)
    tr_shape = jax.ShapeDtypeStruct((batch, NSA_KV_HEADS, V_ROWS, seq), BF16)
    return pl.pallas_call(
        _kv_kernel,
        grid=(batch, per_b),
        in_specs=[
            pl.BlockSpec((ROW_TILE, D_MODEL), lambda b, j: (b * per_b + j, 0)),
            pl.BlockSpec((1, 2, D_MODEL), lambda b, j: (b, 0, 0)),
            _const_spec((1, D_MODEL)),
            _const_spec((D_MODEL, 4 * w)),
            _const_spec((2 * w, D_MODEL)),
            _const_spec((w, w)),
            _const_spec((1, w)),
            _const_spec((1, w)),
            pl.BlockSpec((ROW_TILE, LANES), lambda b, j: (j, 0)),
            pl.BlockSpec((ROW_TILE, LANES), lambda b, j: (j, 0)),
        ],
        out_specs=[nat_spec, nat_spec, nat_spec, nat_spec, tr_spec, tr_spec],
        out_shape=[nat_shape, nat_shape, nat_shape, nat_shape, tr_shape, tr_shape],
        compiler_params=_params(("arbitrary", "arbitrary")),
        name="nsa_kv",
    )(x2, mod, gain, w_nat, w_vt, seg, gsel, gwin, cosn, sinn)


def _compress_pre(r_ref, w1_ref, pos_ref):
    r = r_ref[0, 0]
    half = CMP_STRIDE * HEAD_DIM
    w1a = w1_ref[0:half, :]
    w1b = w1_ref[half:2 * half, :]
    n_rows = r.shape[0]
    u = _dot(r, w1a)
    v = _dot(r, w1b)
    pos = pos_ref[...]
    ph, pl_ = _split(pos)
    bias = (_dot(ph[:, 0:half], w1a) + _dot(pl_[:, 0:half], w1a)
            + _dot(ph[:, half:], w1b) + _dot(pl_[:, half:], w1b))[0:1, :]
    pre = u + pltpu.roll(v, n_rows - 1, 0) + bias
    return _silu(pre).astype(BF16)


def _compress_k_kernel(r_ref, w1_ref, pos_ref, w2_ref, gain_ref, o_ref):
    hid = _compress_pre(r_ref, w1_ref, pos_ref)
    out = _dot(hid, w2_ref[...])
    ms = jnp.mean(out * out, axis=-1, keepdims=True)
    o_ref[0, 0] = (out * lax.rsqrt(ms + EPS) * gain_ref[...]).astype(BF16)


def _compress_v_kernel(r_ref, w1_ref, pos_ref, w2t_ref, o_ref):
    hid = _compress_pre(r_ref, w1_ref, pos_ref)
    o_ref[0, 0, 0:HEAD_DIM, :] = _dot_nt(w2t_ref[...], hid).astype(BF16)
    o_ref[0, 0, HEAD_DIM:V_ROWS, :] = jnp.ones((V_ROWS - HEAD_DIM, hid.shape[0]), BF16)


def _compress(r, w1, pos8, w2, gain, batch, n_rows, transposed):
    half2 = CMP_BLOCK * HEAD_DIM
    in_specs = [
        pl.BlockSpec((1, 1, n_rows, CMP_STRIDE * HEAD_DIM), lambda b, hd: (b, hd, 0, 0)),
        _const_spec((half2, CMP_HIDDEN)),
        _const_spec((8, half2)),
    ]
    if transposed:
        kern = _compress_v_kernel
        in_specs.append(_const_spec((HEAD_DIM, CMP_HIDDEN)))
        args = (r, w1, pos8, w2)
        out_spec = pl.BlockSpec((1, 1, V_ROWS, n_rows), lambda b, hd: (b, hd, 0, 0))
        out_shape = jax.ShapeDtypeStruct((batch, NSA_KV_HEADS, V_ROWS, n_rows), BF16)
    else:
        kern = _compress_k_kernel
        in_specs += [_const_spec((CMP_HIDDEN, HEAD_DIM)), _const_spec((1, HEAD_DIM))]
        args = (r, w1, pos8, w2, gain)
        out_spec = pl.BlockSpec((1, 1, n_rows, HEAD_DIM), lambda b, hd: (b, hd, 0, 0))
        out_shape = jax.ShapeDtypeStruct((batch, NSA_KV_HEADS, n_rows, HEAD_DIM), BF16)
    return pl.pallas_call(
        kern,
        grid=(batch, NSA_KV_HEADS),
        in_specs=in_specs,
        out_specs=out_spec,
        out_shape=out_shape,
        compiler_params=_params(("arbitrary", "arbitrary")),
        name="nsa_compress_v" if transposed else "nsa_compress_k",
    )(*args)


def _q_kernel(x_ref, mod_ref, gain_ref, wq_ref, wgt_ref, seg_ref, qgain_ref, cos_ref, sin_ref,
              qn_ref, qr_ref, gt_ref):
    x = x_ref[...]
    m = mod_ref[0]
    h = _mod_norm(x, gain_ref[...], m[0:1], m[1:2]).astype(BF16)
    gt_ref[0] = _sigmoid(_dot_nt(wgt_ref[...], h))
    seg = seg_ref[...]
    cosn = cos_ref[...]
    sinn = sin_ref[...]
    scale = HEAD_DIM ** -0.5 * LOG2E
    w = NSA_GROUP * HEAD_DIM
    for c in range(NSA_KV_HEADS):
        qc = _dot(h, wq_ref[:, c * w:(c + 1) * w])
        qn = _segnorm64(qc, seg, qgain_ref[...]) * scale
        qr = _rope_wide(qn, cosn, sinn)
        qn = qn.astype(BF16)
        qr = qr.astype(BF16)
        for g in range(NSA_GROUP):
            sl = slice(g * HEAD_DIM, (g + 1) * HEAD_DIM)
            qn_ref[0, c * NSA_GROUP + g] = qn[:, sl]
            qr_ref[0, c * NSA_GROUP + g] = qr[:, sl]


def _q_proj(x2, mod, gain, w_q, w_gt, seg, qgain, cosn, sinn, batch, seq):
    per_b = seq // ROW_TILE
    w = NSA_GROUP * HEAD_DIM
    n_gate = N_BRANCH * NSA_HEADS
    q_spec = pl.BlockSpec((1, NSA_HEADS, ROW_TILE, HEAD_DIM), lambda b, j: (b, 0, j, 0))
    q_shape = jax.ShapeDtypeStruct((batch, NSA_HEADS, seq, HEAD_DIM), BF16)
    return pl.pallas_call(
        _q_kernel,
        grid=(batch, per_b),
        in_specs=[
            pl.BlockSpec((ROW_TILE, D_MODEL), lambda b, j: (b * per_b + j, 0)),
            pl.BlockSpec((1, 3, D_MODEL), lambda b, j: (b, 0, 0)),
            _const_spec((1, D_MODEL)),
            _const_spec((D_MODEL, D_MODEL)),
            _const_spec((n_gate, D_MODEL)),
            _const_spec((w, w)),
            _const_spec((1, w)),
            pl.BlockSpec((ROW_TILE, LANES), lambda b, j: (j, 0)),
            pl.BlockSpec((ROW_TILE, LANES), lambda b, j: (j, 0)),
        ],
        out_specs=[q_spec, q_spec, pl.BlockSpec((1, n_gate, ROW_TILE), lambda b, j: (b, 0, j))],
        out_shape=[q_shape, q_shape, jax.ShapeDtypeStruct((batch, n_gate, seq), F32)],
        compiler_params=_params(("arbitrary", "arbitrary")),
        name="nsa_q",
    )(x2, mod, gain, w_q, w_gt, seg, qgain, cosn, sinn)


def _attn_kernel(qn_ref, qr_ref, gt_ref, kc_ref, vct_ref, ks_ref, vst_ref, kw_ref, vwt_ref, ovl_ref,
                 o_ref, sel_scr, oc_scr, s_a, s_b, p_a, p_b, *, n_cmp, n_sel):
    qb = Q_TILE
    ng = NSA_GROUP
    hk = pl.program_id(1)
    qi = pl.program_id(2)
    t0 = qi * qb
    q_n = qn_ref[0].reshape(ng * qb, HEAD_DIM)
    q_r = qr_ref[0].reshape(ng * qb, HEAD_DIM)

    def tq(rows, cols):
        lane = lax.broadcasted_iota(jnp.int32, (rows, cols), 1)
        return t0 + (lane & (qb - 1))

    n_win = WINDOW + qb
    w0 = pl.multiple_of(jnp.maximum(t0 - WINDOW, 0), qb)
    s_w = _dot_nt(kw_ref[0, 0, pl.ds(w0, n_win), :], q_r)
    kpos = w0 + lax.broadcasted_iota(jnp.int32, (n_win, qb), 0)
    t_w = tq(n_win, qb)
    bias_w = jnp.where((kpos <= t_w) & (kpos > t_w - WINDOW), 0.0, NEG_INF)
    sb_w = s_w + jnp.concatenate([bias_w] * ng, axis=1)
    p_w = jnp.exp2(sb_w - jnp.max(sb_w, axis=0, keepdims=True)).astype(BF16)
    acc_w = _dot(vwt_ref[0, 0, :, pl.ds(w0, n_win)], p_w)

    def compress_and_select(n_c, n_s):
        c_end = lax.broadcasted_iota(jnp.int32, (n_c, qb), 0) * CMP_STRIDE + (CMP_BLOCK - 1)
        bias_c = jnp.where(c_end <= tq(n_c, qb), 0.0, NEG_INF)
        sb_c = _dot_nt(kc_ref[0, 0, 0:n_c, :], q_n) + jnp.concatenate([bias_c] * ng, axis=1)
        m_c = jnp.max(sb_c, axis=0, keepdims=True)
        e_c = jnp.exp2(sb_c - m_c).astype(BF16)
        acc_c = _dot(vct_ref[0, 0, :, 0:n_c], e_c)
        inv_c = jnp.where(m_c > 0.5 * NEG_INF, 1.0 / acc_c[HEAD_DIM:HEAD_DIM + 1], 0.0)
        oc_scr[...] = acc_c[0:HEAD_DIM] * inv_c

        imp_g = _dot(ovl_ref[0:n_s, 0:n_c], e_c) * inv_c
        imp = imp_g[:, 0:qb]
        for g in range(1, ng):
            imp = imp + imp_g[:, g * qb:(g + 1) * qb]
        j_i = lax.broadcasted_iota(jnp.int32, (n_s, qb), 0)
        cur = (t0 + lax.broadcasted_iota(jnp.int32, (n_s, qb), 1)) // SEL_BLOCK
        valid = j_i <= cur
        forced = (j_i == 0) | (j_i == cur) | (j_i == cur - 1)
        val = jnp.where(valid & jnp.logical_not(forced), imp, -1.0)
        for _ in range(N_SELECT - 3):
            mx = jnp.max(val, axis=0, keepdims=True)
            idx = jnp.min(jnp.where(val == mx, j_i, n_s), axis=0, keepdims=True)
            val = jnp.where((j_i == idx) & (mx >= 0.0), -3.0e38, val)
        chosen = (forced & valid) | (val < -1.0e38)
        sel_scr[0:n_s, :] = jnp.where(chosen, 0.0, NEG_INF)
        if n_s < n_sel:
            sel_scr[n_s:n_sel, :] = jnp.full((n_sel - n_s, qb), NEG_INF, F32)

    variant = (t0 + qb - 1) // (n_sel * SEL_BLOCK // CAUSAL_VARIANTS)
    for v in range(CAUSAL_VARIANTS):
        pl.when(variant == v)(functools.partial(
            compress_and_select, (v + 1) * n_cmp // CAUSAL_VARIANTS, (v + 1) * n_sel // CAUSAL_VARIANTS))
    o_c = oc_scr[...]

    n_blk = SEL_BIG // SEL_BLOCK
    last_tile = n_sel // n_blk - 1

    def qk(kt, s_ref):
        k0 = pl.multiple_of(kt * SEL_BIG, SEL_BIG)
        blk = kt * n_blk + lax.broadcasted_iota(jnp.int32, (n_blk, qb), 0)
        rows = jnp.where(blk < t0 // SEL_BLOCK, sel_scr[pl.ds(kt * n_blk, n_blk), :], NEG_INF)
        bias = jnp.concatenate(
            [jnp.broadcast_to(rows[i:i + 1, :], (SEL_BLOCK, qb)) for i in range(n_blk)], axis=0)
        sb = _dot_nt(ks_ref[0, 0, pl.ds(k0, SEL_BIG), :], q_r) + jnp.concatenate([bias] * ng, axis=1)
        s_ref[...] = sb
        return jnp.max(sb, axis=0, keepdims=True)

    def soft(s_ref, p_ref, m_old, tile_max):
        m_new = jnp.maximum(m_old, tile_max)
        p_ref[...] = jnp.exp2(s_ref[...] - m_new).astype(BF16)
        return m_new, jnp.exp2(m_old - m_new)

    def pv(kt, p_ref, alpha, acc):
        k0 = pl.multiple_of(kt * SEL_BIG, SEL_BIG)
        return alpha * acc + _dot(vst_ref[0, 0, :, pl.ds(k0, SEL_BIG)], p_ref[...])

    def pair(j, carry):
        m, alpha_prev, acc, max_a = carry
        ta = 2 * j
        max_b = qk(ta + 1, s_b)
        m, alpha_a = soft(s_a, p_a, m, max_a)
        acc = pv(jnp.maximum(ta - 1, 0), p_b, alpha_prev, acc)
        max_a = qk(jnp.minimum(ta + 2, last_tile), s_a)
        m, alpha_b = soft(s_b, p_b, m, max_b)
        acc = pv(ta, p_a, alpha_a, acc)
        return m, alpha_b, acc, max_a

    n_tiles = (t0 + SEL_BIG - 1) // SEL_BIG
    n_pairs = n_tiles // 2
    p_b[...] = jnp.zeros(p_b.shape, BF16)
    m_s, alpha_s, acc_s, max_a = lax.fori_loop(
        0, n_pairs, pair,
        (jnp.full((1, ng * qb), NEG_INF, F32), jnp.ones((1, ng * qb), F32), jnp.zeros((V_ROWS, ng * qb), F32),
         qk(0, s_a)))
    acc_s = pv(jnp.maximum(2 * n_pairs - 1, 0), p_b, alpha_s, acc_s)

    def odd_tile(args):
        m, acc = args
        m, alpha = soft(s_a, p_a, m, max_a)
        return m, pv(n_tiles - 1, p_a, alpha, acc)

    m_s, acc_s = lax.cond(n_tiles % 2 == 1, odd_tile, lambda args: args, (m_s, acc_s))

    d0 = pl.multiple_of(t0, qb)
    r_d = lax.broadcasted_iota(jnp.int32, (qb, qb), 0)
    c_d = lax.broadcasted_iota(jnp.int32, (qb, qb), 1)
    rows_d = _load_rows(lambda r: sel_scr[pl.ds(r, 8), :], t0 // SEL_BLOCK, qb // SEL_BLOCK)
    chosen_d = jnp.concatenate(
        [jnp.broadcast_to(rows_d[i:i + 1, :], (SEL_BLOCK, qb)) for i in range(qb // SEL_BLOCK)], axis=0)
    sb_d = (_dot_nt(ks_ref[0, 0, pl.ds(d0, qb), :], q_r)
            + jnp.concatenate([jnp.where(r_d <= c_d, chosen_d, NEG_INF)] * ng, axis=1))
    m_d = jnp.maximum(m_s, jnp.max(sb_d, axis=0, keepdims=True))
    acc_s = jnp.exp2(m_s - m_d) * acc_s + _dot(vst_ref[0, 0, :, pl.ds(d0, qb)], jnp.exp2(sb_d - m_d).astype(BF16))

    o_s = acc_s[0:HEAD_DIM] * (1.0 / acc_s[HEAD_DIM:HEAD_DIM + 1])
    o_w = acc_w[0:HEAD_DIM] * (1.0 / acc_w[HEAD_DIM:HEAD_DIM + 1])
    g_c, g_s, g_w = [_load_rows(lambda r: gt_ref[0, pl.ds(r, 8), :], br * NSA_HEADS + hk * ng, ng)
                     for br in range(N_BRANCH)]
    for g in range(ng):
        sl = slice(g * qb, (g + 1) * qb)
        mix = g_c[g:g + 1, :] * o_c[:, sl] + g_s[g:g + 1, :] * o_s[:, sl] + g_w[g:g + 1, :] * o_w[:, sl]
        o_ref[0, g * HEAD_DIM:(g + 1) * HEAD_DIM, :] = mix.astype(BF16)


def _attention(qn, qr, gt, kc, vct, ks, vst, kw, vwt, ovl_t, batch, seq):
    assert seq % (2 * SEL_BIG) == 0 and seq >= WINDOW + Q_TILE and seq // SEL_BLOCK >= N_SELECT * CAUSAL_VARIANTS
    n_cmp = seq // CMP_STRIDE
    n_sel = seq // SEL_BLOCK
    qb = Q_TILE
    n_gate = N_BRANCH * NSA_HEADS
    q_spec = pl.BlockSpec((1, NSA_GROUP, qb, HEAD_DIM), lambda b, hk, i: (b, hk, i, 0))
    nat_full = pl.BlockSpec((1, 1, seq, HEAD_DIM), lambda b, hk, i: (b, hk, 0, 0))
    tr_full = pl.BlockSpec((1, 1, V_ROWS, seq), lambda b, hk, i: (b, hk, 0, 0))
    return pl.pallas_call(
        functools.partial(_attn_kernel, n_cmp=n_cmp, n_sel=n_sel),
        grid=(batch, NSA_KV_HEADS, seq // qb),
        in_specs=[
            q_spec, q_spec,
            pl.BlockSpec((1, n_gate, qb), lambda b, hk, i: (b, 0, i)),
            pl.BlockSpec((1, 1, n_cmp, HEAD_DIM), lambda b, hk, i: (b, hk, 0, 0)),
            pl.BlockSpec((1, 1, V_ROWS, n_cmp), lambda b, hk, i: (b, hk, 0, 0)),
            nat_full, tr_full, nat_full, tr_full,
            _const_spec((n_sel, n_cmp)),
        ],
        out_specs=pl.BlockSpec((1, NSA_GROUP * HEAD_DIM, qb), lambda b, hk, i: (b, hk, i)),
        out_shape=jax.ShapeDtypeStruct((batch, D_MODEL, seq), BF16),
        scratch_shapes=[
            pltpu.VMEM((n_sel, qb), F32),
            pltpu.VMEM((HEAD_DIM, NSA_GROUP * qb), F32),
            pltpu.VMEM((SEL_BIG, NSA_GROUP * qb), F32),
            pltpu.VMEM((SEL_BIG, NSA_GROUP * qb), F32),
            pltpu.VMEM((SEL_BIG, NSA_GROUP * qb), BF16),
            pltpu.VMEM((SEL_BIG, NSA_GROUP * qb), BF16),
        ],
        compiler_params=_params(("arbitrary", "arbitrary", "arbitrary")),
        name="nsa_attention",
    )(qn, qr, gt, kc, vct, ks, vst, kw, vwt, ovl_t)


def _rope_lane_tables(seq):
    inv_freq = 1.0 / (ROPE_THETA ** (jnp.arange(0, HEAD_DIM, 2, dtype=F32) / HEAD_DIM))
    ang = jnp.arange(seq, dtype=F32)[:, None] * inv_freq[None, :]
    cos, sin = jnp.cos(ang), jnp.sin(ang)
    return jnp.tile(cos, (1, 4)), jnp.concatenate([-sin, sin, -sin, sin], axis=-1)


def _overlap_t(n_sel, n_cmp):
    cs = jnp.arange(n_cmp)[None, :] * CMP_STRIDE
    ss = jnp.arange(n_sel)[:, None] * SEL_BLOCK
    return ((cs < ss + SEL_BLOCK) & (cs + CMP_BLOCK > ss)).astype(BF16)


def kernel(x, c, ada_w, ada_b, norm_mix, norm_ffn, hgrn_w_in, hgrn_lower_bounds, hgrn_out_norm, hgrn_w_out,
           kv_ada_w, kv_ada_b, kv_norm, nsa_w_kv, nsa_k_norm, cmp_pos, cmp_w1, cmp_w2, nsa_w_q, nsa_q_norm,
           nsa_w_out, ffn_w_in, ffn_w_out):
    batch, seq, _ = x.shape
    w = NSA_KV_HEADS * HEAD_DIM
    x2 = x.reshape(batch * seq, D_MODEL)

    c_pad = jnp.zeros((8, D_MODEL), F32).at[:batch].set(c)
    mods = _mods(c_pad, ada_w, ada_b)[:, :batch].reshape(2 * DEPTH, batch, 3, D_MODEL)
    kv_mod = _mods(c_pad, kv_ada_w[None], kv_ada_b[None])[0, :batch].reshape(batch, 2, D_MODEL)

    cosn, sinn = _rope_lane_tables(seq)
    eye_seg = jnp.kron(jnp.eye(w // HEAD_DIM, dtype=F32), jnp.ones((HEAD_DIM, HEAD_DIM), F32)).astype(BF16)
    n_rows = seq // CMP_STRIDE
    ovl_t = _overlap_t(seq // SEL_BLOCK, n_rows)
    shared = None

    for layer in range(DEPTH):
        mix_mod = mods[2 * layer]
        gain = norm_mix[layer][None, :]
        if layer < N_A_LAYERS:
            x2 = _hgrn(x2, mix_mod, gain, hgrn_w_in[layer].astype(BF16), hgrn_lower_bounds,
                       hgrn_out_norm[layer][None, :], hgrn_w_out[layer].astype(BF16), batch, seq, layer)
        else:
            if shared is None:
                wkv = nsa_w_kv.reshape(D_MODEL, 2 * N_BRANCH, w)
                w_nat = jnp.concatenate([wkv[:, 0], wkv[:, 1], wkv[:, 2], wkv[:, 4]], axis=1).astype(BF16)
                w_vt = jnp.concatenate([wkv[:, 3], wkv[:, 5]], axis=1).T.astype(BF16)
                kcmp, vcmp, ks, kw, vst, vwt = _kv_prep(
                    x2, kv_mod, kv_norm[None, :], w_nat, w_vt, eye_seg,
                    jnp.tile(nsa_k_norm[1], NSA_KV_HEADS)[None, :], jnp.tile(nsa_k_norm[2], NSA_KV_HEADS)[None, :],
                    cosn, sinn, batch, seq)
                pos8 = jnp.zeros((2, 8, CMP_BLOCK * HEAD_DIM), F32).at[:, 0].set(
                    cmp_pos.reshape(2, CMP_BLOCK * HEAD_DIM))
                rk = kcmp.reshape(batch, NSA_KV_HEADS, n_rows, CMP_STRIDE * HEAD_DIM)
                rv = vcmp.reshape(batch, NSA_KV_HEADS, n_rows, CMP_STRIDE * HEAD_DIM)
                kc = _compress(rk, cmp_w1[0].astype(BF16), pos8[0], cmp_w2[0].astype(BF16),
                               nsa_k_norm[0][None, :], batch, n_rows, False)
                vct = _compress(rv, cmp_w1[1].astype(BF16), pos8[1], cmp_w2[1].T.astype(BF16),
                                None, batch, n_rows, True)
                shared = (kc, vct, ks, vst, kw, vwt)
            bl = layer - N_A_LAYERS
            wq = nsa_w_q[bl]
            qn, qr, gt = _q_proj(x2, mix_mod, gain, wq[:, :D_MODEL].astype(BF16), wq[:, D_MODEL:].T.astype(BF16),
                                 eye_seg, jnp.tile(nsa_q_norm[bl], NSA_GROUP)[None, :], cosn, sinn, batch, seq)
            o_t = _attention(qn, qr, gt, *shared, ovl_t, batch, seq)
        ffn_args = (mods[2 * layer + 1], norm_ffn[layer][None, :], ffn_w_in[layer].astype(BF16),
                    ffn_w_out[layer].astype(BF16))
        if layer < N_A_LAYERS:
            x2 = _ffn(x2, *ffn_args, seq)
        else:
            x2 = _out_ffn(x2, mix_mod, o_t, nsa_w_out[bl].astype(BF16), *ffn_args, batch, seq)
    return x2.reshape(batch, seq, D_MODEL)
```

```python
import functools

import jax
import jax.numpy as jnp
from jax import lax
from jax.experimental import pallas as pl
from jax.experimental.pallas import tpu as pltpu

F32 = jnp.float32
BF16 = jnp.bfloat16

D_MODEL = 1024
DEPTH = 4
N_A_LAYERS = DEPTH // 2
HGRN_HEADS = 8
HGRN_DK = 128
NSA_HEADS = 16
NSA_KV_HEADS = 4
NSA_GROUP = 4
HEAD_DIM = 64
N_BRANCH = 3
CMP_BLOCK = 32
CMP_STRIDE = 16
CMP_HIDDEN = 256
SEL_BLOCK = 64
N_SELECT = 16
WINDOW = 512
ROPE_THETA = 10000.0
FFN_HIDDEN = 2816
EPS = 1e-6
NEG_INF = -1e30
FORCE_SCORE = 1e9

LANES = 128
MXU_N = 256
VMEM_LIMIT = 56 * 1024 * 1024

ROW_TILE = 512
HGRN_CHUNK = 128
Q_TILE = 256
SEL_KEYS = 256
SEL_BIG = 1024
FFN_CHUNK = 256
V_ROWS = HEAD_DIM + 16
LOG2E = 1.4426950408889634
CAUSAL_VARIANTS = 8


def _params(sem):
    return pltpu.CompilerParams(dimension_semantics=sem, vmem_limit_bytes=VMEM_LIMIT)


def _const_spec(shape):
    nd = len(shape)
    return pl.BlockSpec(shape, lambda *_: (0,) * nd, pipeline_mode=pl.Buffered(1))


def _dot(a, b):
    return jnp.dot(a, b, preferred_element_type=F32)


def _dot_nt(a, b):
    return lax.dot_general(a, b, (((1,), (1,)), ((), ())), preferred_element_type=F32)


def _dot_tn(a, b):
    return lax.dot_general(a, b, (((0,), (0,)), ((), ())), preferred_element_type=F32)


def _split(a):
    hi = a.astype(BF16)
    lo = (a - hi.astype(F32)).astype(BF16)
    return hi, lo


def _sigmoid(x):
    return 1.0 / (1.0 + jnp.exp(-x))


def _silu(x):
    return x * _sigmoid(x)


def _mod_norm(x, gain, shift, scale):
    ms = jnp.mean(x * x, axis=-1, keepdims=True)
    y = x * lax.rsqrt(ms + EPS) * gain
    return y * (1.0 + scale) + shift


def _mods_kernel(c_ref, w_ref, b_ref, o_ref):
    c = c_ref[...]
    ah, al = _split(_silu(c))
    wh, wl = _split(w_ref[0])
    o_ref[0] = _dot(ah, wh) + _dot(al, wh) + _dot(ah, wl) + b_ref[0]


def _mods(c_pad, w, b):
    n_l, _, n = w.shape
    tn = 1024
    return pl.pallas_call(
        _mods_kernel,
        grid=(n_l, n // tn),
        in_specs=[
            pl.BlockSpec((8, D_MODEL), lambda l, j: (0, 0)),
            pl.BlockSpec((1, D_MODEL, tn), lambda l, j: (l, 0, j)),
            pl.BlockSpec((1, 1, tn), lambda l, j: (l, 0, j)),
        ],
        out_specs=pl.BlockSpec((1, 8, tn), lambda l, j: (l, 0, j)),
        out_shape=jax.ShapeDtypeStruct((n_l, 8, n), F32),
        compiler_params=_params(("arbitrary", "arbitrary")),
        name="adaln_mods",
    )(c_pad, w, b.reshape(n_l, 1, n))


def _ffn_body(x, m, gain, win_ref, wout_ref):
    h = _mod_norm(x, gain, m[0:1], m[1:2]).astype(BF16)
    acc = jnp.zeros(x.shape, F32)
    for c in range(FFN_HIDDEN // FFN_CHUNK):
        lo = c * FFN_CHUNK
        a = _dot(h, win_ref[:, lo:lo + FFN_CHUNK])
        b = _dot(h, win_ref[:, FFN_HIDDEN + lo:FFN_HIDDEN + lo + FFN_CHUNK])
        g = (_silu(a) * b).astype(BF16)
        acc = acc + _dot(g, wout_ref[lo:lo + FFN_CHUNK, :])
    return x + m[2:3] * acc


def _ffn_kernel(x_ref, mod_ref, gain_ref, win_ref, wout_ref, o_ref):
    o_ref[...] = _ffn_body(x_ref[...], mod_ref[0], gain_ref[...], win_ref, wout_ref)


def _out_ffn_kernel(x_ref, mixmod_ref, ot_ref, wo_ref, mod_ref, gain_ref, win_ref, wout_ref, o_ref):
    x = x_ref[...] + mixmod_ref[0][2:3] * _dot_tn(ot_ref[0], wo_ref[...])
    o_ref[...] = _ffn_body(x, mod_ref[0], gain_ref[...], win_ref, wout_ref)


def _out_ffn(x2, mix_mod, o_t, w_o, mod, gain, w_in, w_out, batch, seq):
    per_b = seq // ROW_TILE
    row_spec = pl.BlockSpec((ROW_TILE, D_MODEL), lambda b, j: (b * per_b + j, 0))
    mod_spec = pl.BlockSpec((1, 3, D_MODEL), lambda b, j: (b, 0, 0))
    return pl.pallas_call(
        _out_ffn_kernel,
        grid=(batch, per_b),
        in_specs=[
            row_spec, mod_spec,
            pl.BlockSpec((1, D_MODEL, ROW_TILE), lambda b, j: (b, 0, j)),
            _const_spec((D_MODEL, D_MODEL)),
            mod_spec,
            _const_spec((1, D_MODEL)),
            _const_spec((D_MODEL, 2 * FFN_HIDDEN)),
            _const_spec((FFN_HIDDEN, D_MODEL)),
        ],
        out_specs=row_spec,
        out_shape=jax.ShapeDtypeStruct(x2.shape, F32),
        compiler_params=_params(("arbitrary", "arbitrary")),
        name="nsa_out_ffn",
    )(x2, mix_mod, o_t, w_o, mod, gain, w_in, w_out)


def _ffn(x2, mod, gain, w_in, w_out, seq):
    t = x2.shape[0]
    per_b = seq // ROW_TILE
    return pl.pallas_call(
        _ffn_kernel,
        grid=(t // ROW_TILE,),
        in_specs=[
            pl.BlockSpec((ROW_TILE, D_MODEL), lambda i: (i, 0)),
            pl.BlockSpec((1, 3, D_MODEL), lambda i: (i // per_b, 0, 0)),
            _const_spec((1, D_MODEL)),
            _const_spec((D_MODEL, 2 * FFN_HIDDEN)),
            _const_spec((FFN_HIDDEN, D_MODEL)),
        ],
        out_specs=pl.BlockSpec((ROW_TILE, D_MODEL), lambda i: (i, 0)),
        out_shape=jax.ShapeDtypeStruct(x2.shape, F32),
        compiler_params=_params(("arbitrary",)),
        name="ffn",
    )(x2, mod, gain, w_in, w_out)


def _hgrn_kernel(x_ref, mod_ref, gain_ref, win_ref, lbraw_ref, onorm_ref, wout_ref,
                 o_ref, proj_scr, st_scr, oall_scr, *, layer):
    tc = HGRN_CHUNK
    dk = HGRN_DK

    @pl.when(pl.program_id(1) == 0)
    def _():
        st_scr[...] = jnp.zeros(st_scr.shape, F32)

    x = x_ref[...]
    m = mod_ref[0]
    h = _mod_norm(x, gain_ref[...], m[0:1], m[1:2]).astype(BF16)
    proj_scr[...] = _dot(h, win_ref[...])

    if layer > 0:
        raw = lbraw_ref[...]
        e = jnp.exp(raw - jnp.max(raw, axis=0, keepdims=True))
        sm = e / jnp.sum(e, axis=0, keepdims=True)
        lb = jnp.sum(sm[1:layer + 1], axis=0, keepdims=True)

    row = lax.broadcasted_iota(jnp.int32, (tc, D_MODEL), 0)
    r_i = lax.broadcasted_iota(jnp.int32, (tc, tc), 0)
    c_i = lax.broadcasted_iota(jnp.int32, (tc, tc), 1)
    ones_b = jnp.ones((dk, tc), BF16)
    tri = (r_i >= c_i).astype(BF16)
    n_lvl = tc.bit_length() - 1

    def chunk(ci, carry):
        r0 = pl.multiple_of(ci * tc, tc)
        qp = proj_scr[pl.ds(r0, tc), 0:D_MODEL]
        fp = proj_scr[pl.ds(r0, tc), D_MODEL:2 * D_MODEL]
        v = proj_scr[pl.ds(r0, tc), 2 * D_MODEL:3 * D_MODEL].astype(BF16)
        gp = proj_scr[pl.ds(r0, tc), 3 * D_MODEL:4 * D_MODEL]

        q = _silu(qp)
        e = jnp.exp(-jnp.abs(fp))
        r = 1.0 / (1.0 + e)
        pos = fp >= 0.0
        sig = jnp.where(pos, r, e * r)
        nsig = jnp.where(pos, e * r, r)
        if layer == 0:
            logf = jnp.minimum(fp, 0.0) - jnp.log(1.0 + e)
            kk = nsig
        else:
            logf = jnp.log(lb + (1.0 - lb) * sig)
            kk = (1.0 - lb) * nsig

        g_inc = logf * LOG2E
        g_hi = g_inc.astype(BF16)
        g_res = g_inc - g_hi.astype(F32)
        g_mid = g_res.astype(BF16)
        g_lo = (g_res - g_mid.astype(F32)).astype(BF16)
        g_cum = _dot(tri, g_hi) + _dot(tri, g_mid) + _dot(tri, g_lo)

        p_acc = [jnp.zeros((tc, tc), F32) for _ in range(HGRN_HEADS)]
        end_val = g_cum
        for lvl in range(n_lvl):
            hs = 1 << lvl
            n_b = tc // hs
            if hs < 8:
                second = (row & hs) != 0
                ref_val = jnp.where(second, pltpu.roll(end_val, hs, 0), end_val)
                nxt_val = jnp.where(second, end_val, pltpu.roll(end_val, tc - hs, 0))
            else:
                blocks = [end_val[i * hs:(i + 1) * hs] for i in range(n_b)]
                ref_val = jnp.concatenate([blocks[i - i % 2] for i in range(n_b)], axis=0)
                nxt_val = jnp.concatenate([blocks[i - i % 2 + 1] for i in range(n_b)], axis=0)
            dist = lax.bitcast_convert_type(g_cum - ref_val, jnp.uint32) | jnp.uint32(0x80000000)
            decay = jnp.exp2(lax.bitcast_convert_type(dist, F32))
            qs = (q * decay).astype(BF16)
            ks = (kk * decay).astype(BF16)
            upper = (((r_i >> lvl) - (c_i >> lvl)) == 1) & (((c_i >> lvl) & 1) == 0)
            for hd in range(HGRN_HEADS):
                sl = slice(hd * dk, (hd + 1) * dk)
                p_acc[hd] = p_acc[hd] + jnp.where(upper, _dot_nt(qs[:, sl], ks[:, sl]), 0.0)
            end_val = nxt_val

        qk = (q * kk).astype(BF16)
        g_last = g_cum[tc - 1:tc, :]
        qe = (q * jnp.exp2(g_cum)).astype(BF16)
        kd = (kk * jnp.exp2(g_last - g_cum)).astype(BF16)
        s_decay = jnp.exp2(g_last)
        gate_act = _silu(gp)
        onorm = onorm_ref[...]
        for hd in range(HGRN_HEADS):
            sl = slice(hd * dk, (hd + 1) * dk)
            p_h = p_acc[hd] + jnp.where(r_i == c_i, _dot(qk[:, sl], ones_b), 0.0)
            st = st_scr[hd]
            o_h = _dot(p_h.astype(BF16), v[:, sl]) + _dot_nt(qe[:, sl], st.astype(BF16))
            st_scr[hd] = s_decay[:, sl] * st + _dot_tn(v[:, sl], kd[:, sl])
            ms = jnp.mean(o_h * o_h, axis=-1, keepdims=True)
            o_n = o_h * lax.rsqrt(ms + EPS) * onorm
            oall_scr[pl.ds(r0, tc), sl] = (o_n * gate_act[:, sl]).astype(BF16)
        return carry

    lax.fori_loop(0, ROW_TILE // tc, chunk, 0)
    o_ref[...] = x + m[2:3] * _dot(oall_scr[...], wout_ref[...])


def _hgrn(x2, mod, gain, w_in, lb_raw, onorm, w_out, batch, seq, layer):
    per_b = seq // ROW_TILE
    return pl.pallas_call(
        functools.partial(_hgrn_kernel, layer=layer),
        grid=(batch, per_b),
        in_specs=[
            pl.BlockSpec((ROW_TILE, D_MODEL), lambda b, j: (b * per_b + j, 0)),
            pl.BlockSpec((1, 3, D_MODEL), lambda b, j: (b, 0, 0)),
            _const_spec((1, D_MODEL)),
            _const_spec((D_MODEL, 4 * D_MODEL)),
            _const_spec((N_A_LAYERS, D_MODEL)),
            _const_spec((1, HGRN_DK)),
            _const_spec((D_MODEL, D_MODEL)),
        ],
        out_specs=pl.BlockSpec((ROW_TILE, D_MODEL), lambda b, j: (b * per_b + j, 0)),
        out_shape=jax.ShapeDtypeStruct(x2.shape, F32),
        scratch_shapes=[
            pltpu.VMEM((ROW_TILE, 4 * D_MODEL), F32),
            pltpu.VMEM((HGRN_HEADS, HGRN_DK, HGRN_DK), F32),
            pltpu.VMEM((ROW_TILE, D_MODEL), BF16),
        ],
        compiler_params=_params(("arbitrary", "arbitrary")),
        name=f"hgrn{layer}",
    )(x2, mod, gain, w_in, lb_raw, onorm, w_out)


def _segnorm64(xc, seg_ones, gain):
    hi, lo = _split(xc * xc)
    ss = _dot(hi, seg_ones) + _dot(lo, seg_ones)
    return xc * lax.rsqrt(ss * (1.0 / HEAD_DIM) + EPS) * gain


def _load_rows(load8, start, n):
    blk = load8(pl.multiple_of(start // 8 * 8, 8))
    off = start % 8
    out = blk[0:n]
    for o in range(1, 8 // n):
        out = jnp.where(off == o * n, blk[o * n:(o + 1) * n], out)
    return out


def _rope_lanes(xp, cosn, sinn):
    lane = lax.broadcasted_iota(jnp.int32, xp.shape, 1)
    first = (lane & (HEAD_DIM // 2)) == 0
    rot = jnp.where(first, pltpu.roll(xp, LANES - HEAD_DIM // 2, 1), pltpu.roll(xp, HEAD_DIM // 2, 1))
    return xp * cosn + rot * sinn


def _rope_wide(xc, cosn, sinn):
    return jnp.concatenate(
        [_rope_lanes(xc[:, i * LANES:(i + 1) * LANES], cosn, sinn) for i in range(xc.shape[1] // LANES)], axis=1)


def _kv_kernel(x_ref, mod_ref, gain_ref, wn_ref, wvt_ref, seg_ref, gsel_ref, gwin_ref, cos_ref, sin_ref,
               kcmp_ref, vcmp_ref, ksel_ref, kwin_ref, vselt_ref, vwint_ref):
    x = x_ref[...]
    m = mod_ref[0]
    h = _mod_norm(x, gain_ref[...], m[0:1], m[1:2]).astype(BF16)
    nat = _dot(h, wn_ref[...])
    vt = _dot_nt(wvt_ref[...], h)
    w = NSA_KV_HEADS * HEAD_DIM
    seg = seg_ref[...]
    cosn = cos_ref[...]
    sinn = sin_ref[...]
    kcmp = nat[:, 0:w].astype(BF16)
    vcmp = nat[:, w:2 * w].astype(BF16)
    ksel = _rope_wide(_segnorm64(nat[:, 2 * w:3 * w], seg, gsel_ref[...]), cosn, sinn).astype(BF16)
    kwin = _rope_wide(_segnorm64(nat[:, 3 * w:4 * w], seg, gwin_ref[...]), cosn, sinn).astype(BF16)
    ones = jnp.ones((V_ROWS - HEAD_DIM, x.shape[0]), BF16)
    for hd in range(NSA_KV_HEADS):
        sl = slice(hd * HEAD_DIM, (hd + 1) * HEAD_DIM)
        kcmp_ref[0, hd] = kcmp[:, sl]
        vcmp_ref[0, hd] = vcmp[:, sl]
        ksel_ref[0, hd] = ksel[:, sl]
        kwin_ref[0, hd] = kwin[:, sl]
        vselt_ref[0, hd, 0:HEAD_DIM, :] = vt[hd * HEAD_DIM:(hd + 1) * HEAD_DIM, :].astype(BF16)
        vselt_ref[0, hd, HEAD_DIM:V_ROWS, :] = ones
        vwint_ref[0, hd, 0:HEAD_DIM, :] = vt[w + hd * HEAD_DIM:w + (hd + 1) * HEAD_DIM, :].astype(BF16)
        vwint_ref[0, hd, HEAD_DIM:V_ROWS, :] = ones


def _kv_prep(x2, mod, gain, w_nat, w_vt, seg, gsel, gwin, cosn, sinn, batch, seq):
    per_b = seq // ROW_TILE
    w = NSA_KV_HEADS * HEAD_DIM
    nat_spec = pl.BlockSpec((1, NSA_KV_HEADS, ROW_TILE, HEAD_DIM), lambda b, j: (b, 0, j, 0))
    tr_spec = pl.BlockSpec((1, NSA_KV_HEADS, V_ROWS, ROW_TILE), lambda b, j: (b, 0, 0, j))
    nat_shape = jax.ShapeDtypeStruct((batch, NSA_KV_HEADS, seq, HEAD_DIM), BF16)
    tr_shape = jax.ShapeDtypeStruct((batch, NSA_KV_HEADS, V_ROWS, seq), BF16)
    return pl.pallas_call(
        _kv_kernel,
        grid=(batch, per_b),
        in_specs=[
            pl.BlockSpec((ROW_TILE, D_MODEL), lambda b, j: (b * per_b + j, 0)),
            pl.BlockSpec((1, 2, D_MODEL), lambda b, j: (b, 0, 0)),
            _const_spec((1, D_MODEL)),
            _const_spec((D_MODEL, 4 * w)),
            _const_spec((2 * w, D_MODEL)),
            _const_spec((w, w)),
            _const_spec((1, w)),
            _const_spec((1, w)),
            pl.BlockSpec((ROW_TILE, LANES), lambda b, j: (j, 0)),
            pl.BlockSpec((ROW_TILE, LANES), lambda b, j: (j, 0)),
        ],
        out_specs=[nat_spec, nat_spec, nat_spec, nat_spec, tr_spec, tr_spec],
        out_shape=[nat_shape, nat_shape, nat_shape, nat_shape, tr_shape, tr_shape],
        compiler_params=_params(("arbitrary", "arbitrary")),
        name="nsa_kv",
    )(x2, mod, gain, w_nat, w_vt, seg, gsel, gwin, cosn, sinn)


def _compress_pre(r_ref, w1_ref, pos_ref):
    r = r_ref[0, 0]
    half = CMP_STRIDE * HEAD_DIM
    w1a = w1_ref[0:half, :]
    w1b = w1_ref[half:2 * half, :]
    n_rows = r.shape[0]
    u = _dot(r, w1a)
    v = _dot(r, w1b)
    pos = pos_ref[...]
    ph, pl_ = _split(pos)
    bias = (_dot(ph[:, 0:half], w1a) + _dot(pl_[:, 0:half], w1a)
            + _dot(ph[:, half:], w1b) + _dot(pl_[:, half:], w1b))[0:1, :]
    pre = u + pltpu.roll(v, n_rows - 1, 0) + bias
    return _silu(pre).astype(BF16)


def _compress_k_kernel(r_ref, w1_ref, pos_ref, w2_ref, gain_ref, o_ref):
    hid = _compress_pre(r_ref, w1_ref, pos_ref)
    out = _dot(hid, w2_ref[...])
    ms = jnp.mean(out * out, axis=-1, keepdims=True)
    o_ref[0, 0] = (out * lax.rsqrt(ms + EPS) * gain_ref[...]).astype(BF16)


def _compress_v_kernel(r_ref, w1_ref, pos_ref, w2t_ref, o_ref):
    hid = _compress_pre(r_ref, w1_ref, pos_ref)
    o_ref[0, 0, 0:HEAD_DIM, :] = _dot_nt(w2t_ref[...], hid).astype(BF16)
    o_ref[0, 0, HEAD_DIM:V_ROWS, :] = jnp.ones((V_ROWS - HEAD_DIM, hid.shape[0]), BF16)


def _compress(r, w1, pos8, w2, gain, batch, n_rows, transposed):
    half2 = CMP_BLOCK * HEAD_DIM
    in_specs = [
        pl.BlockSpec((1, 1, n_rows, CMP_STRIDE * HEAD_DIM), lambda b, hd: (b, hd, 0, 0)),
        _const_spec((half2, CMP_HIDDEN)),
        _const_spec((8, half2)),
    ]
    if transposed:
        kern = _compress_v_kernel
        in_specs.append(_const_spec((HEAD_DIM, CMP_HIDDEN)))
        args = (r, w1, pos8, w2)
        out_spec = pl.BlockSpec((1, 1, V_ROWS, n_rows), lambda b, hd: (b, hd, 0, 0))
        out_shape = jax.ShapeDtypeStruct((batch, NSA_KV_HEADS, V_ROWS, n_rows), BF16)
    else:
        kern = _compress_k_kernel
        in_specs += [_const_spec((CMP_HIDDEN, HEAD_DIM)), _const_spec((1, HEAD_DIM))]
        args = (r, w1, pos8, w2, gain)
        out_spec = pl.BlockSpec((1, 1, n_rows, HEAD_DIM), lambda b, hd: (b, hd, 0, 0))
        out_shape = jax.ShapeDtypeStruct((batch, NSA_KV_HEADS, n_rows, HEAD_DIM), BF16)
    return pl.pallas_call(
        kern,
        grid=(batch, NSA_KV_HEADS),
        in_specs=in_specs,
        out_specs=out_spec,
        out_shape=out_shape,
        compiler_params=_params(("arbitrary", "arbitrary")),
        name="nsa_compress_v" if transposed else "nsa_compress_k",
    )(*args)


def _q_kernel(x_ref, mod_ref, gain_ref, wq_ref, wgt_ref, seg_ref, qgain_ref, cos_ref, sin_ref,
              qn_ref, qr_ref, gt_ref):
    x = x_ref[...]
    m = mod_ref[0]
    h = _mod_norm(x, gain_ref[...], m[0:1], m[1:2]).astype(BF16)
    gt_ref[0] = _sigmoid(_dot_nt(wgt_ref[...], h))
    seg = seg_ref[...]
    cosn = cos_ref[...]
    sinn = sin_ref[...]
    scale = HEAD_DIM ** -0.5 * LOG2E
    w = NSA_GROUP * HEAD_DIM
    for c in range(NSA_KV_HEADS):
        qc = _dot(h, wq_ref[:, c * w:(c + 1) * w])
        qn = _segnorm64(qc, seg, qgain_ref[...]) * scale
        qr = _rope_wide(qn, cosn, sinn)
        qn = qn.astype(BF16)
        qr = qr.astype(BF16)
        for g in range(NSA_GROUP):
            sl = slice(g * HEAD_DIM, (g + 1) * HEAD_DIM)
            qn_ref[0, c * NSA_GROUP + g] = qn[:, sl]
            qr_ref[0, c * NSA_GROUP + g] = qr[:, sl]


def _q_proj(x2, mod, gain, w_q, w_gt, seg, qgain, cosn, sinn, batch, seq):
    per_b = seq // ROW_TILE
    w = NSA_GROUP * HEAD_DIM
    n_gate = N_BRANCH * NSA_HEADS
    q_spec = pl.BlockSpec((1, NSA_HEADS, ROW_TILE, HEAD_DIM), lambda b, j: (b, 0, j, 0))
    q_shape = jax.ShapeDtypeStruct((batch, NSA_HEADS, seq, HEAD_DIM), BF16)
    return pl.pallas_call(
        _q_kernel,
        grid=(batch, per_b),
        in_specs=[
            pl.BlockSpec((ROW_TILE, D_MODEL), lambda b, j: (b * per_b + j, 0)),
            pl.BlockSpec((1, 3, D_MODEL), lambda b, j: (b, 0, 0)),
            _const_spec((1, D_MODEL)),
            _const_spec((D_MODEL, D_MODEL)),
            _const_spec((n_gate, D_MODEL)),
            _const_spec((w, w)),
            _const_spec((1, w)),
            pl.BlockSpec((ROW_TILE, LANES), lambda b, j: (j, 0)),
            pl.BlockSpec((ROW_TILE, LANES), lambda b, j: (j, 0)),
        ],
        out_specs=[q_spec, q_spec, pl.BlockSpec((1, n_gate, ROW_TILE), lambda b, j: (b, 0, j))],
        out_shape=[q_shape, q_shape, jax.ShapeDtypeStruct((batch, n_gate, seq), F32)],
        compiler_params=_params(("arbitrary", "arbitrary")),
        name="nsa_q",
    )(x2, mod, gain, w_q, w_gt, seg, qgain, cosn, sinn)


def _attn_kernel(qn_ref, qr_ref, gt_ref, kc_ref, vct_ref, ks_ref, vst_ref, kw_ref, vwt_ref, ovl_ref,
                 o_ref, sel_scr, oc_scr, s_a, s_b, p_a, p_b, *, n_cmp, n_sel):
    qb = Q_TILE
    ng = NSA_GROUP
    hk = pl.program_id(1)
    qi = pl.program_id(2)
    t0 = qi * qb
    q_n = qn_ref[0].reshape(ng * qb, HEAD_DIM)
    q_r = qr_ref[0].reshape(ng * qb, HEAD_DIM)

    def tq(rows, cols):
        lane = lax.broadcasted_iota(jnp.int32, (rows, cols), 1)
        return t0 + (lane & (qb - 1))

    n_win = WINDOW + qb
    w0 = pl.multiple_of(jnp.maximum(t0 - WINDOW, 0), qb)
    s_w = _dot_nt(kw_ref[0, 0, pl.ds(w0, n_win), :], q_r)
    kpos = w0 + lax.broadcasted_iota(jnp.int32, (n_win, qb), 0)
    t_w = tq(n_win, qb)
    bias_w = jnp.where((kpos <= t_w) & (kpos > t_w - WINDOW), 0.0, NEG_INF)
    sb_w = s_w + jnp.concatenate([bias_w] * ng, axis=1)
    p_w = jnp.exp2(sb_w - jnp.max(sb_w, axis=0, keepdims=True)).astype(BF16)
    acc_w = _dot(vwt_ref[0, 0, :, pl.ds(w0, n_win)], p_w)

    def compress_and_select(n_c, n_s):
        c_end = lax.broadcasted_iota(jnp.int32, (n_c, qb), 0) * CMP_STRIDE + (CMP_BLOCK - 1)
        bias_c = jnp.where(c_end <= tq(n_c, qb), 0.0, NEG_INF)
        sb_c = _dot_nt(kc_ref[0, 0, 0:n_c, :], q_n) + jnp.concatenate([bias_c] * ng, axis=1)
        m_c = jnp.max(sb_c, axis=0, keepdims=True)
        e_c = jnp.exp2(sb_c - m_c).astype(BF16)
        acc_c = _dot(vct_ref[0, 0, :, 0:n_c], e_c)
        inv_c = jnp.where(m_c > 0.5 * NEG_INF, 1.0 / acc_c[HEAD_DIM:HEAD_DIM + 1], 0.0)
        oc_scr[...] = acc_c[0:HEAD_DIM] * inv_c

        imp_g = _dot(ovl_ref[0:n_s, 0:n_c], e_c) * inv_c
        imp = imp_g[:, 0:qb]
        for g in range(1, ng):
            imp = imp + imp_g[:, g * qb:(g + 1) * qb]
        j_i = lax.broadcasted_iota(jnp.int32, (n_s, qb), 0)
        cur = (t0 + lax.broadcasted_iota(jnp.int32, (n_s, qb), 1)) // SEL_BLOCK
        valid = j_i <= cur
        forced = (j_i == 0) | (j_i == cur) | (j_i == cur - 1)
        val = jnp.where(valid & jnp.logical_not(forced), imp, -1.0)
        for _ in range(N_SELECT - 3):
            mx = jnp.max(val, axis=0, keepdims=True)
            idx = jnp.min(jnp.where(val == mx, j_i, n_s), axis=0, keepdims=True)
            val = jnp.where((j_i == idx) & (mx >= 0.0), -3.0e38, val)
        chosen = (forced & valid) | (val < -1.0e38)
        sel_scr[0:n_s, :] = jnp.where(chosen, 0.0, NEG_INF)
        if n_s < n_sel:
            sel_scr[n_s:n_sel, :] = jnp.full((n_sel - n_s, qb), NEG_INF, F32)

    n_var = min(CAUSAL_VARIANTS, n_cmp // LANES)
    variant = (t0 + qb - 1) // (n_sel * SEL_BLOCK // n_var)
    for v in range(n_var):
        pl.when(variant == v)(functools.partial(compress_and_select, (v + 1) * n_cmp // n_var, (v + 1) * n_sel // n_var))
    o_c = oc_scr[...]

    n_blk = SEL_BIG // SEL_BLOCK
    last_tile = n_sel // n_blk - 1

    def qk(kt, s_ref):
        k0 = pl.multiple_of(kt * SEL_BIG, SEL_BIG)
        blk = kt * n_blk + lax.broadcasted_iota(jnp.int32, (n_blk, qb), 0)
        rows = jnp.where(blk < t0 // SEL_BLOCK, sel_scr[pl.ds(kt * n_blk, n_blk), :], NEG_INF)
        bias = jnp.concatenate(
            [jnp.broadcast_to(rows[i:i + 1, :], (SEL_BLOCK, qb)) for i in range(n_blk)], axis=0)
        sb = _dot_nt(ks_ref[0, 0, pl.ds(k0, SEL_BIG), :], q_r) + jnp.concatenate([bias] * ng, axis=1)
        s_ref[...] = sb
        return jnp.max(sb, axis=0, keepdims=True)

    def soft(s_ref, p_ref, m_old, tile_max):
        m_new = jnp.maximum(m_old, tile_max)
        p_ref[...] = jnp.exp2(s_ref[...] - m_new).astype(BF16)
        return m_new, jnp.exp2(m_old - m_new)

    def pv(kt, p_ref, alpha, acc):
        k0 = pl.multiple_of(kt * SEL_BIG, SEL_BIG)
        return alpha * acc + _dot(vst_ref[0, 0, :, pl.ds(k0, SEL_BIG)], p_ref[...])

    def pair(j, carry):
        m, alpha_prev, acc, max_a = carry
        ta = 2 * j
        max_b = qk(ta + 1, s_b)
        m, alpha_a = soft(s_a, p_a, m, max_a)
        acc = pv(jnp.maximum(ta - 1, 0), p_b, alpha_prev, acc)
        max_a = qk(jnp.minimum(ta + 2, last_tile), s_a)
        m, alpha_b = soft(s_b, p_b, m, max_b)
        acc = pv(ta, p_a, alpha_a, acc)
        return m, alpha_b, acc, max_a

    n_tiles = (t0 + SEL_BIG - 1) // SEL_BIG
    n_pairs = n_tiles // 2
    p_b[...] = jnp.zeros(p_b.shape, BF16)
    m_s, alpha_s, acc_s, max_a = lax.fori_loop(
        0, n_pairs, pair,
        (jnp.full((1, ng * qb), NEG_INF, F32), jnp.ones((1, ng * qb), F32), jnp.zeros((V_ROWS, ng * qb), F32),
         qk(0, s_a)))
    acc_s = pv(jnp.maximum(2 * n_pairs - 1, 0), p_b, alpha_s, acc_s)

    def odd_tile(args):
        m, acc = args
        m, alpha = soft(s_a, p_a, m, max_a)
        return m, pv(n_tiles - 1, p_a, alpha, acc)

    m_s, acc_s = lax.cond(n_tiles % 2 == 1, odd_tile, lambda args: args, (m_s, acc_s))

    d0 = pl.multiple_of(t0, qb)
    r_d = lax.broadcasted_iota(jnp.int32, (qb, qb), 0)
    c_d = lax.broadcasted_iota(jnp.int32, (qb, qb), 1)
    rows_d = _load_rows(lambda r: sel_scr[pl.ds(r, 8), :], t0 // SEL_BLOCK, qb // SEL_BLOCK)
    chosen_d = jnp.concatenate(
        [jnp.broadcast_to(rows_d[i:i + 1, :], (SEL_BLOCK, qb)) for i in range(qb // SEL_BLOCK)], axis=0)
    sb_d = (_dot_nt(ks_ref[0, 0, pl.ds(d0, qb), :], q_r)
            + jnp.concatenate([jnp.where(r_d <= c_d, chosen_d, NEG_INF)] * ng, axis=1))
    m_d = jnp.maximum(m_s, jnp.max(sb_d, axis=0, keepdims=True))
    acc_s = jnp.exp2(m_s - m_d) * acc_s + _dot(vst_ref[0, 0, :, pl.ds(d0, qb)], jnp.exp2(sb_d - m_d).astype(BF16))

    o_s = acc_s[0:HEAD_DIM] * (1.0 / acc_s[HEAD_DIM:HEAD_DIM + 1])
    o_w = acc_w[0:HEAD_DIM] * (1.0 / acc_w[HEAD_DIM:HEAD_DIM + 1])
    g_c, g_s, g_w = [_load_rows(lambda r: gt_ref[0, pl.ds(r, 8), :], br * NSA_HEADS + hk * ng, ng)
                     for br in range(N_BRANCH)]
    for g in range(ng):
        sl = slice(g * qb, (g + 1) * qb)
        mix = g_c[g:g + 1, :] * o_c[:, sl] + g_s[g:g + 1, :] * o_s[:, sl] + g_w[g:g + 1, :] * o_w[:, sl]
        o_ref[0, g * HEAD_DIM:(g + 1) * HEAD_DIM, :] = mix.astype(BF16)


def _attention(qn, qr, gt, kc, vct, ks, vst, kw, vwt, ovl_t, batch, seq):
    assert seq % (2 * SEL_BIG) == 0 and seq >= WINDOW + Q_TILE and seq // SEL_BLOCK >= N_SELECT
    n_cmp = seq // CMP_STRIDE
    n_sel = seq // SEL_BLOCK
    qb = Q_TILE
    n_gate = N_BRANCH * NSA_HEADS
    q_spec = pl.BlockSpec((1, NSA_GROUP, qb, HEAD_DIM), lambda b, hk, i: (b, hk, i, 0))
    nat_full = pl.BlockSpec((1, 1, seq, HEAD_DIM), lambda b, hk, i: (b, hk, 0, 0))
    tr_full = pl.BlockSpec((1, 1, V_ROWS, seq), lambda b, hk, i: (b, hk, 0, 0))
    return pl.pallas_call(
        functools.partial(_attn_kernel, n_cmp=n_cmp, n_sel=n_sel),
        grid=(batch, NSA_KV_HEADS, seq // qb),
        in_specs=[
            q_spec, q_spec,
            pl.BlockSpec((1, n_gate, qb), lambda b, hk, i: (b, 0, i)),
            pl.BlockSpec((1, 1, n_cmp, HEAD_DIM), lambda b, hk, i: (b, hk, 0, 0)),
            pl.BlockSpec((1, 1, V_ROWS, n_cmp), lambda b, hk, i: (b, hk, 0, 0)),
            nat_full, tr_full, nat_full, tr_full,
            _const_spec((n_sel, n_cmp)),
        ],
        out_specs=pl.BlockSpec((1, NSA_GROUP * HEAD_DIM, qb), lambda b, hk, i: (b, hk, i)),
        out_shape=jax.ShapeDtypeStruct((batch, D_MODEL, seq), BF16),
        scratch_shapes=[
            pltpu.VMEM((n_sel, qb), F32),
            pltpu.VMEM((HEAD_DIM, NSA_GROUP * qb), F32),
            pltpu.VMEM((SEL_BIG, NSA_GROUP * qb), F32),
            pltpu.VMEM((SEL_BIG, NSA_GROUP * qb), F32),
            pltpu.VMEM((SEL_BIG, NSA_GROUP * qb), BF16),
            pltpu.VMEM((SEL_BIG, NSA_GROUP * qb), BF16),
        ],
        compiler_params=_params(("arbitrary", "arbitrary", "arbitrary")),
        name="nsa_attention",
    )(qn, qr, gt, kc, vct, ks, vst, kw, vwt, ovl_t)


def _rope_lane_tables(seq):
    inv_freq = 1.0 / (ROPE_THETA ** (jnp.arange(0, HEAD_DIM, 2, dtype=F32) / HEAD_DIM))
    ang = jnp.arange(seq, dtype=F32)[:, None] * inv_freq[None, :]
    cos, sin = jnp.cos(ang), jnp.sin(ang)
    return jnp.tile(cos, (1, 4)), jnp.concatenate([-sin, sin, -sin, sin], axis=-1)


def _overlap_t(n_sel, n_cmp):
    cs = jnp.arange(n_cmp)[None, :] * CMP_STRIDE
    ss = jnp.arange(n_sel)[:, None] * SEL_BLOCK
    return ((cs < ss + SEL_BLOCK) & (cs + CMP_BLOCK > ss)).astype(BF16)


def kernel(x, c, ada_w, ada_b, norm_mix, norm_ffn, hgrn_w_in, hgrn_lower_bounds, hgrn_out_norm, hgrn_w_out,
           kv_ada_w, kv_ada_b, kv_norm, nsa_w_kv, nsa_k_norm, cmp_pos, cmp_w1, cmp_w2, nsa_w_q, nsa_q_norm,
           nsa_w_out, ffn_w_in, ffn_w_out):
    batch, seq, _ = x.shape
    w = NSA_KV_HEADS * HEAD_DIM
    x2 = x.reshape(batch * seq, D_MODEL)

    c_pad = jnp.zeros((8, D_MODEL), F32).at[:batch].set(c)
    mods = _mods(c_pad, ada_w, ada_b)[:, :batch].reshape(2 * DEPTH, batch, 3, D_MODEL)
    kv_mod = _mods(c_pad, kv_ada_w[None], kv_ada_b[None])[0, :batch].reshape(batch, 2, D_MODEL)

    cosn, sinn = _rope_lane_tables(seq)
    eye_seg = jnp.kron(jnp.eye(w // HEAD_DIM, dtype=F32), jnp.ones((HEAD_DIM, HEAD_DIM), F32)).astype(BF16)
    n_rows = seq // CMP_STRIDE
    ovl_t = _overlap_t(seq // SEL_BLOCK, n_rows)
    shared = None

    for layer in range(DEPTH):
        mix_mod = mods[2 * layer]
        gain = norm_mix[layer][None, :]
        if layer < N_A_LAYERS:
            x2 = _hgrn(x2, mix_mod, gain, hgrn_w_in[layer].astype(BF16), hgrn_lower_bounds,
                       hgrn_out_norm[layer][None, :], hgrn_w_out[layer].astype(BF16), batch, seq, layer)
        else:
            if shared is None:
                wkv = nsa_w_kv.reshape(D_MODEL, 2 * N_BRANCH, w)
                w_nat = jnp.concatenate([wkv[:, 0], wkv[:, 1], wkv[:, 2], wkv[:, 4]], axis=1).astype(BF16)
                w_vt = jnp.concatenate([wkv[:, 3], wkv[:, 5]], axis=1).T.astype(BF16)
                kcmp, vcmp, ks, kw, vst, vwt = _kv_prep(
                    x2, kv_mod, kv_norm[None, :], w_nat, w_vt, eye_seg,
                    jnp.tile(nsa_k_norm[1], NSA_KV_HEADS)[None, :], jnp.tile(nsa_k_norm[2], NSA_KV_HEADS)[None, :],
                    cosn, sinn, batch, seq)
                pos8 = jnp.zeros((2, 8, CMP_BLOCK * HEAD_DIM), F32).at[:, 0].set(
                    cmp_pos.reshape(2, CMP_BLOCK * HEAD_DIM))
                rk = kcmp.reshape(batch, NSA_KV_HEADS, n_rows, CMP_STRIDE * HEAD_DIM)
                rv = vcmp.reshape(batch, NSA_KV_HEADS, n_rows, CMP_STRIDE * HEAD_DIM)
                kc = _compress(rk, cmp_w1[0].astype(BF16), pos8[0], cmp_w2[0].astype(BF16),
                               nsa_k_norm[0][None, :], batch, n_rows, False)
                vct = _compress(rv, cmp_w1[1].astype(BF16), pos8[1], cmp_w2[1].T.astype(BF16),
                                None, batch, n_rows, True)
                shared = (kc, vct, ks, vst, kw, vwt)
            bl = layer - N_A_LAYERS
            wq = nsa_w_q[bl]
            qn, qr, gt = _q_proj(x2, mix_mod, gain, wq[:, :D_MODEL].astype(BF16), wq[:, D_MODEL:].T.astype(BF16),
                                 eye_seg, jnp.tile(nsa_q_norm[bl], NSA_GROUP)[None, :], cosn, sinn, batch, seq)
            o_t = _attention(qn, qr, gt, *shared, ovl_t, batch, seq)
        ffn_args = (mods[2 * layer + 1], norm_ffn[layer][None, :], ffn_w_in[layer].astype(BF16),
                    ffn_w_out[layer].astype(BF16))
        if layer < N_A_LAYERS:
            x2 = _ffn(x2, *ffn_args, seq)
        else:
            x2 = _out_ffn(x2, mix_mod, o_t, nsa_w_out[bl].astype(BF16), *ffn_args, batch, seq)
    return x2.reshape(batch, seq, D_MODEL)
```

```python
import functools

import jax
import jax.numpy as jnp
from jax import lax
from jax.experimental import pallas as pl
from jax.experimental.pallas import tpu as pltpu

F32 = jnp.float32
BF16 = jnp.bfloat16

D_MODEL = 1024
DEPTH = 4
N_A_LAYERS = DEPTH // 2
HGRN_HEADS = 8
HGRN_DK = 128
NSA_HEADS = 16
NSA_KV_HEADS = 4
NSA_GROUP = 4
HEAD_DIM = 64
N_BRANCH = 3
CMP_BLOCK = 32
CMP_STRIDE = 16
CMP_HIDDEN = 256
SEL_BLOCK = 64
N_SELECT = 16
WINDOW = 512
ROPE_THETA = 10000.0
FFN_HIDDEN = 2816
EPS = 1e-6
NEG_INF = -1e30
FORCE_SCORE = 1e9

LANES = 128
MXU_N = 256
VMEM_LIMIT = 56 * 1024 * 1024

ROW_TILE = 512
HGRN_CHUNK = 128
Q_TILE = 256
SEL_KEYS = 256
SEL_BIG = 1024
FFN_CHUNK = 256
V_ROWS = HEAD_DIM + 16
LOG2E = 1.4426950408889634
CAUSAL_VARIANTS = 8


def _params(sem):
    return pltpu.CompilerParams(dimension_semantics=sem, vmem_limit_bytes=VMEM_LIMIT)


def _const_spec(shape):
    nd = len(shape)
    return pl.BlockSpec(shape, lambda *_: (0,) * nd, pipeline_mode=pl.Buffered(1))


def _dot(a, b):
    return jnp.dot(a, b, preferred_element_type=F32)


def _dot_nt(a, b):
    return lax.dot_general(a, b, (((1,), (1,)), ((), ())), preferred_element_type=F32)


def _dot_tn(a, b):
    return lax.dot_general(a, b, (((0,), (0,)), ((), ())), preferred_element_type=F32)


def _split(a):
    hi = a.astype(BF16)
    lo = (a - hi.astype(F32)).astype(BF16)
    return hi, lo


def _sigmoid(x):
    return 1.0 / (1.0 + jnp.exp(-x))


def _silu(x):
    return x * _sigmoid(x)


def _mod_norm(x, gain, shift, scale):
    ms = jnp.mean(x * x, axis=-1, keepdims=True)
    y = x * lax.rsqrt(ms + EPS) * gain
    return y * (1.0 + scale) + shift


def _mods_kernel(c_ref, w_ref, b_ref, o_ref):
    c = c_ref[...]
    ah, al = _split(_silu(c))
    wh, wl = _split(w_ref[0])
    o_ref[0] = _dot(ah, wh) + _dot(al, wh) + _dot(ah, wl) + b_ref[0]


def _mods(c_pad, w, b):
    n_l, _, n = w.shape
    tn = 1024
    return pl.pallas_call(
        _mods_kernel,
        grid=(n_l, n // tn),
        in_specs=[
            pl.BlockSpec((8, D_MODEL), lambda l, j: (0, 0)),
            pl.BlockSpec((1, D_MODEL, tn), lambda l, j: (l, 0, j)),
            pl.BlockSpec((1, 1, tn), lambda l, j: (l, 0, j)),
        ],
        out_specs=pl.BlockSpec((1, 8, tn), lambda l, j: (l, 0, j)),
        out_shape=jax.ShapeDtypeStruct((n_l, 8, n), F32),
        compiler_params=_params(("arbitrary", "arbitrary")),
        name="adaln_mods",
    )(c_pad, w, b.reshape(n_l, 1, n))


def _ffn_body(x, m, gain, win_ref, wout_ref):
    h = _mod_norm(x, gain, m[0:1], m[1:2]).astype(BF16)
    acc = jnp.zeros(x.shape, F32)
    for c in range(FFN_HIDDEN // FFN_CHUNK):
        lo = c * FFN_CHUNK
        a = _dot(h, win_ref[:, lo:lo + FFN_CHUNK])
        b = _dot(h, win_ref[:, FFN_HIDDEN + lo:FFN_HIDDEN + lo + FFN_CHUNK])
        g = (_silu(a) * b).astype(BF16)
        acc = acc + _dot(g, wout_ref[lo:lo + FFN_CHUNK, :])
    return x + m[2:3] * acc


def _ffn_kernel(x_ref, mod_ref, gain_ref, win_ref, wout_ref, o_ref):
    o_ref[...] = _ffn_body(x_ref[...], mod_ref[0], gain_ref[...], win_ref, wout_ref)


def _out_ffn_kernel(x_ref, mixmod_ref, ot_ref, wo_ref, mod_ref, gain_ref, win_ref, wout_ref, o_ref):
    x = x_ref[...] + mixmod_ref[0][2:3] * _dot_tn(ot_ref[0], wo_ref[...])
    o_ref[...] = _ffn_body(x, mod_ref[0], gain_ref[...], win_ref, wout_ref)


def _out_ffn(x2, mix_mod, o_t, w_o, mod, gain, w_in, w_out, batch, seq):
    per_b = seq // ROW_TILE
    row_spec = pl.BlockSpec((ROW_TILE, D_MODEL), lambda b, j: (b * per_b + j, 0))
    mod_spec = pl.BlockSpec((1, 3, D_MODEL), lambda b, j: (b, 0, 0))
    return pl.pallas_call(
        _out_ffn_kernel,
        grid=(batch, per_b),
        in_specs=[
            row_spec, mod_spec,
            pl.BlockSpec((1, D_MODEL, ROW_TILE), lambda b, j: (b, 0, j)),
            _const_spec((D_MODEL, D_MODEL)),
            mod_spec,
            _const_spec((1, D_MODEL)),
            _const_spec((D_MODEL, 2 * FFN_HIDDEN)),
            _const_spec((FFN_HIDDEN, D_MODEL)),
        ],
        out_specs=row_spec,
        out_shape=jax.ShapeDtypeStruct(x2.shape, F32),
        compiler_params=_params(("arbitrary", "arbitrary")),
        name="nsa_out_ffn",
    )(x2, mix_mod, o_t, w_o, mod, gain, w_in, w_out)


def _ffn(x2, mod, gain, w_in, w_out, seq):
    t = x2.shape[0]
    per_b = seq // ROW_TILE
    return pl.pallas_call(
        _ffn_kernel,
        grid=(t // ROW_TILE,),
        in_specs=[
            pl.BlockSpec((ROW_TILE, D_MODEL), lambda i: (i, 0)),
            pl.BlockSpec((1, 3, D_MODEL), lambda i: (i // per_b, 0, 0)),
            _const_spec((1, D_MODEL)),
            _const_spec((D_MODEL, 2 * FFN_HIDDEN)),
            _const_spec((FFN_HIDDEN, D_MODEL)),
        ],
        out_specs=pl.BlockSpec((ROW_TILE, D_MODEL), lambda i: (i, 0)),
        out_shape=jax.ShapeDtypeStruct(x2.shape, F32),
        compiler_params=_params(("arbitrary",)),
        name="ffn",
    )(x2, mod, gain, w_in, w_out)


def _hgrn_kernel(x_ref, mod_ref, gain_ref, win_ref, lbraw_ref, onorm_ref, wout_ref,
                 o_ref, proj_scr, st_scr, oall_scr, *, layer):
    tc = HGRN_CHUNK
    dk = HGRN_DK

    @pl.when(pl.program_id(1) == 0)
    def _():
        st_scr[...] = jnp.zeros(st_scr.shape, F32)

    x = x_ref[...]
    m = mod_ref[0]
    h = _mod_norm(x, gain_ref[...], m[0:1], m[1:2]).astype(BF16)
    proj_scr[...] = _dot(h, win_ref[...])

    if layer > 0:
        raw = lbraw_ref[...]
        e = jnp.exp(raw - jnp.max(raw, axis=0, keepdims=True))
        sm = e / jnp.sum(e, axis=0, keepdims=True)
        lb = jnp.sum(sm[1:layer + 1], axis=0, keepdims=True)

    row = lax.broadcasted_iota(jnp.int32, (tc, D_MODEL), 0)
    r_i = lax.broadcasted_iota(jnp.int32, (tc, tc), 0)
    c_i = lax.broadcasted_iota(jnp.int32, (tc, tc), 1)
    ones_b = jnp.ones((dk, tc), BF16)
    tri = (r_i >= c_i).astype(BF16)
    n_lvl = tc.bit_length() - 1

    def chunk(ci, carry):
        r0 = pl.multiple_of(ci * tc, tc)
        qp = proj_scr[pl.ds(r0, tc), 0:D_MODEL]
        fp = proj_scr[pl.ds(r0, tc), D_MODEL:2 * D_MODEL]
        v = proj_scr[pl.ds(r0, tc), 2 * D_MODEL:3 * D_MODEL].astype(BF16)
        gp = proj_scr[pl.ds(r0, tc), 3 * D_MODEL:4 * D_MODEL]

        q = _silu(qp)
        e = jnp.exp(-jnp.abs(fp))
        r = 1.0 / (1.0 + e)
        pos = fp >= 0.0
        sig = jnp.where(pos, r, e * r)
        nsig = jnp.where(pos, e * r, r)
        if layer == 0:
            logf = jnp.minimum(fp, 0.0) - jnp.log(1.0 + e)
            kk = nsig
        else:
            logf = jnp.log(lb + (1.0 - lb) * sig)
            kk = (1.0 - lb) * nsig

        g_inc = logf * LOG2E
        g_hi = g_inc.astype(BF16)
        g_res = g_inc - g_hi.astype(F32)
        g_mid = g_res.astype(BF16)
        g_lo = (g_res - g_mid.astype(F32)).astype(BF16)
        g_cum = _dot(tri, g_hi) + _dot(tri, g_mid) + _dot(tri, g_lo)

        p_acc = [jnp.zeros((tc, tc), F32) for _ in range(HGRN_HEADS)]
        end_val = g_cum
        for lvl in range(n_lvl):
            hs = 1 << lvl
            n_b = tc // hs
            if hs < 8:
                second = (row & hs) != 0
                ref_val = jnp.where(second, pltpu.roll(end_val, hs, 0), end_val)
                nxt_val = jnp.where(second, end_val, pltpu.roll(end_val, tc - hs, 0))
            else:
                blocks = [end_val[i * hs:(i + 1) * hs] for i in range(n_b)]
                ref_val = jnp.concatenate([blocks[i - i % 2] for i in range(n_b)], axis=0)
                nxt_val = jnp.concatenate([blocks[i - i % 2 + 1] for i in range(n_b)], axis=0)
            dist = lax.bitcast_convert_type(g_cum - ref_val, jnp.uint32) | jnp.uint32(0x80000000)
            decay = jnp.exp2(lax.bitcast_convert_type(dist, F32))
            qs = (q * decay).astype(BF16)
            ks = (kk * decay).astype(BF16)
            upper = (((r_i >> lvl) - (c_i >> lvl)) == 1) & (((c_i >> lvl) & 1) == 0)
            for hd in range(HGRN_HEADS):
                sl = slice(hd * dk, (hd + 1) * dk)
                p_acc[hd] = p_acc[hd] + jnp.where(upper, _dot_nt(qs[:, sl], ks[:, sl]), 0.0)
            end_val = nxt_val

        qk = (q * kk).astype(BF16)
        g_last = g_cum[tc - 1:tc, :]
        qe = (q * jnp.exp2(g_cum)).astype(BF16)
        kd = (kk * jnp.exp2(g_last - g_cum)).astype(BF16)
        s_decay = jnp.exp2(g_last)
        gate_act = _silu(gp)
        onorm = onorm_ref[...]
        for hd in range(HGRN_HEADS):
            sl = slice(hd * dk, (hd + 1) * dk)
            p_h = p_acc[hd] + jnp.where(r_i == c_i, _dot(qk[:, sl], ones_b), 0.0)
            st = st_scr[hd]
            o_h = _dot(p_h.astype(BF16), v[:, sl]) + _dot_nt(qe[:, sl], st.astype(BF16))
            st_scr[hd] = s_decay[:, sl] * st + _dot_tn(v[:, sl], kd[:, sl])
            ms = jnp.mean(o_h * o_h, axis=-1, keepdims=True)
            o_n = o_h * lax.rsqrt(ms + EPS) * onorm
            oall_scr[pl.ds(r0, tc), sl] = (o_n * gate_act[:, sl]).astype(BF16)
        return carry

    lax.fori_loop(0, ROW_TILE // tc, chunk, 0)
    o_ref[...] = x + m[2:3] * _dot(oall_scr[...], wout_ref[...])


def _hgrn(x2, mod, gain, w_in, lb_raw, onorm, w_out, batch, seq, layer):
    per_b = seq // ROW_TILE
    return pl.pallas_call(
        functools.partial(_hgrn_kernel, layer=layer),
        grid=(batch, per_b),
        in_specs=[
            pl.BlockSpec((ROW_TILE, D_MODEL), lambda b, j: (b * per_b + j, 0)),
            pl.BlockSpec((1, 3, D_MODEL), lambda b, j: (b, 0, 0)),
            _const_spec((1, D_MODEL)),
            _const_spec((D_MODEL, 4 * D_MODEL)),
            _const_spec((N_A_LAYERS, D_MODEL)),
            _const_spec((1, HGRN_DK)),
            _const_spec((D_MODEL, D_MODEL)),
        ],
        out_specs=pl.BlockSpec((ROW_TILE, D_MODEL), lambda b, j: (b * per_b + j, 0)),
        out_shape=jax.ShapeDtypeStruct(x2.shape, F32),
        scratch_shapes=[
            pltpu.VMEM((ROW_TILE, 4 * D_MODEL), F32),
            pltpu.VMEM((HGRN_HEADS, HGRN_DK, HGRN_DK), F32),
            pltpu.VMEM((ROW_TILE, D_MODEL), BF16),
        ],
        compiler_params=_params(("arbitrary", "arbitrary")),
        name=f"hgrn{layer}",
    )(x2, mod, gain, w_in, lb_raw, onorm, w_out)


def _segnorm64(xc, seg_ones, gain):
    hi, lo = _split(xc * xc)
    ss = _dot(hi, seg_ones) + _dot(lo, seg_ones)
    return xc * lax.rsqrt(ss * (1.0 / HEAD_DIM) + EPS) * gain


def _load_rows(load8, start, n):
    blk = load8(pl.multiple_of(start // 8 * 8, 8))
    off = start % 8
    out = blk[0:n]
    for o in range(1, 8 // n):
        out = jnp.where(off == o * n, blk[o * n:(o + 1) * n], out)
    return out


def _rope_lanes(xp, cosn, sinn):
    lane = lax.broadcasted_iota(jnp.int32, xp.shape, 1)
    first = (lane & (HEAD_DIM // 2)) == 0
    rot = jnp.where(first, pltpu.roll(xp, LANES - HEAD_DIM // 2, 1), pltpu.roll(xp, HEAD_DIM // 2, 1))
    return xp * cosn + rot * sinn


def _rope_wide(xc, cosn, sinn):
    return jnp.concatenate(
        [_rope_lanes(xc[:, i * LANES:(i + 1) * LANES], cosn, sinn) for i in range(xc.shape[1] // LANES)], axis=1)


def _kv_kernel(x_ref, mod_ref, gain_ref, wn_ref, wvt_ref, seg_ref, gsel_ref, gwin_ref, cos_ref, sin_ref,
               kcmp_ref, vcmp_ref, ksel_ref, kwin_ref, vselt_ref, vwint_ref):
    x = x_ref[...]
    m = mod_ref[0]
    h = _mod_norm(x, gain_ref[...], m[0:1], m[1:2]).astype(BF16)
    nat = _dot(h, wn_ref[...])
    vt = _dot_nt(wvt_ref[...], h)
    w = NSA_KV_HEADS * HEAD_DIM
    seg = seg_ref[...]
    cosn = cos_ref[...]
    sinn = sin_ref[...]
    kcmp = nat[:, 0:w].astype(BF16)
    vcmp = nat[:, w:2 * w].astype(BF16)
    ksel = _rope_wide(_segnorm64(nat[:, 2 * w:3 * w], seg, gsel_ref[...]), cosn, sinn).astype(BF16)
    kwin = _rope_wide(_segnorm64(nat[:, 3 * w:4 * w], seg, gwin_ref[...]), cosn, sinn).astype(BF16)
    ones = jnp.ones((V_ROWS - HEAD_DIM, x.shape[0]), BF16)
    for hd in range(NSA_KV_HEADS):
        sl = slice(hd * HEAD_DIM, (hd + 1) * HEAD_DIM)
        kcmp_ref[0, hd] = kcmp[:, sl]
        vcmp_ref[0, hd] = vcmp[:, sl]
        ksel_ref[0, hd] = ksel[:, sl]
        kwin_ref[0, hd] = kwin[:, sl]
        vselt_ref[0, hd, 0:HEAD_DIM, :] = vt[hd * HEAD_DIM:(hd + 1) * HEAD_DIM, :].astype(BF16)
        vselt_ref[0, hd, HEAD_DIM:V_ROWS, :] = ones
        vwint_ref[0, hd, 0:HEAD_DIM, :] = vt[w + hd * HEAD_DIM:w + (hd + 1) * HEAD_DIM, :].astype(BF16)
        vwint_ref[0, hd, HEAD_DIM:V_ROWS, :] = ones


def _kv_prep(x2, mod, gain, w_nat, w_vt, seg, gsel, gwin, cosn, sinn, batch, seq):
    per_b = seq // ROW_TILE
    w = NSA_KV_HEADS * HEAD_DIM
    nat_spec = pl.BlockSpec((1, NSA_KV_HEADS, ROW_TILE, HEAD_DIM), lambda b, j: (b, 0, j, 0))
    tr_spec = pl.BlockSpec((1, NSA_KV_HEADS, V_ROWS, ROW_TILE), lambda b, j: (b, 0, 0, j))
    nat_shape = jax.ShapeDtypeStruct((batch, NSA_KV_HEADS, seq, HEAD_DIM), BF16)
    tr_shape = jax.ShapeDtypeStruct((batch, NSA_KV_HEADS, V_ROWS, seq), BF16)
    return pl.pallas_call(
        _kv_kernel,
        grid=(batch, per_b),
        in_specs=[
            pl.BlockSpec((ROW_TILE, D_MODEL), lambda b, j: (b * per_b + j, 0)),
            pl.BlockSpec((1, 2, D_MODEL), lambda b, j: (b, 0, 0)),
            _const_spec((1, D_MODEL)),
            _const_spec((D_MODEL, 4 * w)),
            _const_spec((2 * w, D_MODEL)),
            _const_spec((w, w)),
            _const_spec((1, w)),
            _const_spec((1, w)),
            pl.BlockSpec((ROW_TILE, LANES), lambda b, j: (j, 0)),
            pl.BlockSpec((ROW_TILE, LANES), lambda b, j: (j, 0)),
        ],
        out_specs=[nat_spec, nat_spec, nat_spec, nat_spec, tr_spec, tr_spec],
        out_shape=[nat_shape, nat_shape, nat_shape, nat_shape, tr_shape, tr_shape],
        compiler_params=_params(("arbitrary", "arbitrary")),
        name="nsa_kv",
    )(x2, mod, gain, w_nat, w_vt, seg, gsel, gwin, cosn, sinn)


def _compress_pre(r_ref, w1_ref, pos_ref):
    r = r_ref[0, 0]
    half = CMP_STRIDE * HEAD_DIM
    w1a = w1_ref[0:half, :]
    w1b = w1_ref[half:2 * half, :]
    n_rows = r.shape[0]
    u = _dot(r, w1a)
    v = _dot(r, w1b)
    pos = pos_ref[...]
    ph, pl_ = _split(pos)
    bias = (_dot(ph[:, 0:half], w1a) + _dot(pl_[:, 0:half], w1a)
            + _dot(ph[:, half:], w1b) + _dot(pl_[:, half:], w1b))[0:1, :]
    pre = u + pltpu.roll(v, n_rows - 1, 0) + bias
    return _silu(pre).astype(BF16)


def _compress_k_kernel(r_ref, w1_ref, pos_ref, w2_ref, gain_ref, o_ref):
    hid = _compress_pre(r_ref, w1_ref, pos_ref)
    out = _dot(hid, w2_ref[...])
    ms = jnp.mean(out * out, axis=-1, keepdims=True)
    o_ref[0, 0] = (out * lax.rsqrt(ms + EPS) * gain_ref[...]).astype(BF16)


def _compress_v_kernel(r_ref, w1_ref, pos_ref, w2t_ref, o_ref):
    hid = _compress_pre(r_ref, w1_ref, pos_ref)
    o_ref[0, 0, 0:HEAD_DIM, :] = _dot_nt(w2t_ref[...], hid).astype(BF16)
    o_ref[0, 0, HEAD_DIM:V_ROWS, :] = jnp.ones((V_ROWS - HEAD_DIM, hid.shape[0]), BF16)


def _compress(r, w1, pos8, w2, gain, batch, n_rows, transposed):
    half2 = CMP_BLOCK * HEAD_DIM
    in_specs = [
        pl.BlockSpec((1, 1, n_rows, CMP_STRIDE * HEAD_DIM), lambda b, hd: (b, hd, 0, 0)),
        _const_spec((half2, CMP_HIDDEN)),
        _const_spec((8, half2)),
    ]
    if transposed:
        kern = _compress_v_kernel
        in_specs.append(_const_spec((HEAD_DIM, CMP_HIDDEN)))
        args = (r, w1, pos8, w2)
        out_spec = pl.BlockSpec((1, 1, V_ROWS, n_rows), lambda b, hd: (b, hd, 0, 0))
        out_shape = jax.ShapeDtypeStruct((batch, NSA_KV_HEADS, V_ROWS, n_rows), BF16)
    else:
        kern = _compress_k_kernel
        in_specs += [_const_spec((CMP_HIDDEN, HEAD_DIM)), _const_spec((1, HEAD_DIM))]
        args = (r, w1, pos8, w2, gain)
        out_spec = pl.BlockSpec((1, 1, n_rows, HEAD_DIM), lambda b, hd: (b, hd, 0, 0))
        out_shape = jax.ShapeDtypeStruct((batch, NSA_KV_HEADS, n_rows, HEAD_DIM), BF16)
    return pl.pallas_call(
        kern,
        grid=(batch, NSA_KV_HEADS),
        in_specs=in_specs,
        out_specs=out_spec,
        out_shape=out_shape,
        compiler_params=_params(("arbitrary", "arbitrary")),
        name="nsa_compress_v" if transposed else "nsa_compress_k",
    )(*args)


def _q_kernel(x_ref, mod_ref, gain_ref, wq_ref, wgt_ref, seg_ref, qgain_ref, cos_ref, sin_ref,
              qn_ref, qr_ref, gt_ref):
    x = x_ref[...]
    m = mod_ref[0]
    h = _mod_norm(x, gain_ref[...], m[0:1], m[1:2]).astype(BF16)
    gt_ref[0] = _sigmoid(_dot_nt(wgt_ref[...], h))
    seg = seg_ref[...]
    cosn = cos_ref[...]
    sinn = sin_ref[...]
    scale = HEAD_DIM ** -0.5 * LOG2E
    w = NSA_GROUP * HEAD_DIM
    for c in range(NSA_KV_HEADS):
        qc = _dot(h, wq_ref[:, c * w:(c + 1) * w])
        qn = _segnorm64(qc, seg, qgain_ref[...]) * scale
        qr = _rope_wide(qn, cosn, sinn)
        qn = qn.astype(BF16)
        qr = qr.astype(BF16)
        for g in range(NSA_GROUP):
            sl = slice(g * HEAD_DIM, (g + 1) * HEAD_DIM)
            qn_ref[0, c * NSA_GROUP + g] = qn[:, sl]
            qr_ref[0, c * NSA_GROUP + g] = qr[:, sl]


def _q_proj(x2, mod, gain, w_q, w_gt, seg, qgain, cosn, sinn, batch, seq):
    per_b = seq // ROW_TILE
    w = NSA_GROUP * HEAD_DIM
    n_gate = N_BRANCH * NSA_HEADS
    q_spec = pl.BlockSpec((1, NSA_HEADS, ROW_TILE, HEAD_DIM), lambda b, j: (b, 0, j, 0))
    q_shape = jax.ShapeDtypeStruct((batch, NSA_HEADS, seq, HEAD_DIM), BF16)
    return pl.pallas_call(
        _q_kernel,
        grid=(batch, per_b),
        in_specs=[
            pl.BlockSpec((ROW_TILE, D_MODEL), lambda b, j: (b * per_b + j, 0)),
            pl.BlockSpec((1, 3, D_MODEL), lambda b, j: (b, 0, 0)),
            _const_spec((1, D_MODEL)),
            _const_spec((D_MODEL, D_MODEL)),
            _const_spec((n_gate, D_MODEL)),
            _const_spec((w, w)),
            _const_spec((1, w)),
            pl.BlockSpec((ROW_TILE, LANES), lambda b, j: (j, 0)),
            pl.BlockSpec((ROW_TILE, LANES), lambda b, j: (j, 0)),
        ],
        out_specs=[q_spec, q_spec, pl.BlockSpec((1, n_gate, ROW_TILE), lambda b, j: (b, 0, j))],
        out_shape=[q_shape, q_shape, jax.ShapeDtypeStruct((batch, n_gate, seq), F32)],
        compiler_params=_params(("arbitrary", "arbitrary")),
        name="nsa_q",
    )(x2, mod, gain, w_q, w_gt, seg, qgain, cosn, sinn)


def _attn_kernel(qn_ref, qr_ref, gt_ref, kc_ref, vct_ref, ks_ref, vst_ref, kw_ref, vwt_ref, ovl_ref,
                 o_ref, sel_scr, oc_scr, s_a, s_b, p_a, p_b, *, n_cmp, n_sel):
    qb = Q_TILE
    ng = NSA_GROUP
    hk = pl.program_id(1)
    qi = pl.program_id(2)
    t0 = qi * qb
    q_n = qn_ref[0].reshape(ng * qb, HEAD_DIM)
    q_r = qr_ref[0].reshape(ng * qb, HEAD_DIM)

    def tq(rows, cols):
        lane = lax.broadcasted_iota(jnp.int32, (rows, cols), 1)
        return t0 + (lane & (qb - 1))

    n_win = WINDOW + qb
    w0 = pl.multiple_of(jnp.maximum(t0 - WINDOW, 0), qb)
    s_w = _dot_nt(kw_ref[0, 0, pl.ds(w0, n_win), :], q_r)
    kpos = w0 + lax.broadcasted_iota(jnp.int32, (n_win, qb), 0)
    t_w = tq(n_win, qb)
    bias_w = jnp.where((kpos <= t_w) & (kpos > t_w - WINDOW), 0.0, NEG_INF)
    sb_w = s_w + jnp.concatenate([bias_w] * ng, axis=1)
    p_w = jnp.exp2(sb_w - jnp.max(sb_w, axis=0, keepdims=True)).astype(BF16)
    acc_w = _dot(vwt_ref[0, 0, :, pl.ds(w0, n_win)], p_w)

    def compress_and_select(n_c, n_s):
        c_end = lax.broadcasted_iota(jnp.int32, (n_c, qb), 0) * CMP_STRIDE + (CMP_BLOCK - 1)
        bias_c = jnp.where(c_end <= tq(n_c, qb), 0.0, NEG_INF)
        sb_c = _dot_nt(kc_ref[0, 0, 0:n_c, :], q_n) + jnp.concatenate([bias_c] * ng, axis=1)
        m_c = jnp.max(sb_c, axis=0, keepdims=True)
        e_c = jnp.exp2(sb_c - m_c).astype(BF16)
        acc_c = _dot(vct_ref[0, 0, :, 0:n_c], e_c)
        inv_c = jnp.where(m_c > 0.5 * NEG_INF, 1.0 / acc_c[HEAD_DIM:HEAD_DIM + 1], 0.0)
        oc_scr[...] = acc_c[0:HEAD_DIM] * inv_c

        imp_g = _dot(ovl_ref[0:n_s, 0:n_c], e_c) * inv_c
        imp = imp_g[:, 0:qb]
        for g in range(1, ng):
            imp = imp + imp_g[:, g * qb:(g + 1) * qb]
        j_i = lax.broadcasted_iota(jnp.int32, (n_s, qb), 0)
        cur = (t0 + lax.broadcasted_iota(jnp.int32, (n_s, qb), 1)) // SEL_BLOCK
        valid = j_i <= cur
        forced = (j_i == 0) | (j_i == cur) | (j_i == cur - 1)
        val = jnp.where(valid & jnp.logical_not(forced), imp, -1.0)
        for _ in range(N_SELECT - 3):
            mx = jnp.max(val, axis=0, keepdims=True)
            idx = jnp.min(jnp.where(val == mx, j_i, n_s), axis=0, keepdims=True)
            val = jnp.where((j_i == idx) & (mx >= 0.0), -3.0e38, val)
        chosen = (forced & valid) | (val < -1.0e38)
        sel_scr[0:n_s, :] = jnp.where(chosen, 0.0, NEG_INF)
        if n_s < n_sel:
            sel_scr[n_s:n_sel, :] = jnp.full((n_sel - n_s, qb), NEG_INF, F32)

    n_var = min(CAUSAL_VARIANTS, n_cmp // LANES)
    variant = (t0 + qb - 1) // (n_sel * SEL_BLOCK // n_var)
    for v in range(n_var):
        pl.when(variant == v)(functools.partial(compress_and_select, (v + 1) * n_cmp // n_var, (v + 1) * n_sel // n_var))
    o_c = oc_scr[...]

    n_blk = SEL_BIG // SEL_BLOCK

    def qk(kt, s_ref):
        k0 = pl.multiple_of(kt * SEL_BIG, SEL_BIG)
        blk = kt * n_blk + lax.broadcasted_iota(jnp.int32, (n_blk, qb), 0)
        rows = jnp.where(blk < t0 // SEL_BLOCK, sel_scr[pl.ds(kt * n_blk, n_blk), :], NEG_INF)
        bias = jnp.concatenate(
            [jnp.broadcast_to(rows[i:i + 1, :], (SEL_BLOCK, qb)) for i in range(n_blk)], axis=0)
        sb = _dot_nt(ks_ref[0, 0, pl.ds(k0, SEL_BIG), :], q_r) + jnp.concatenate([bias] * ng, axis=1)
        s_ref[...] = sb
        return jnp.max(sb, axis=0, keepdims=True)

    def soft(s_ref, p_ref, m_old, tile_max):
        m_new = jnp.maximum(m_old, tile_max)
        p_ref[...] = jnp.exp2(s_ref[...] - m_new).astype(BF16)
        return m_new, jnp.exp2(m_old - m_new)

    def pv(kt, p_ref, alpha, acc):
        k0 = pl.multiple_of(kt * SEL_BIG, SEL_BIG)
        return alpha * acc + _dot(vst_ref[0, 0, :, pl.ds(k0, SEL_BIG)], p_ref[...])

    def pair(j, carry, look_ahead=True):
        m, alpha_prev, acc, max_a = carry
        ta = 2 * j
        max_b = qk(ta + 1, s_b)
        m, alpha_a = soft(s_a, p_a, m, max_a)
        acc = pv(jnp.maximum(ta - 1, 0), p_b, alpha_prev, acc)
        if look_ahead:
            max_a = qk(ta + 2, s_a)
        m, alpha_b = soft(s_b, p_b, m, max_b)
        acc = pv(ta, p_a, alpha_a, acc)
        return m, alpha_b, acc, max_a

    n_tiles = (t0 + SEL_BIG - 1) // SEL_BIG
    n_pairs = n_tiles // 2
    odd = n_tiles % 2 == 1
    p_b[...] = jnp.zeros(p_b.shape, BF16)
    carry = lax.fori_loop(
        0, jnp.maximum(n_pairs - 1, 0), pair,
        (jnp.full((1, ng * qb), NEG_INF, F32), jnp.ones((1, ng * qb), F32), jnp.zeros((V_ROWS, ng * qb), F32),
         qk(0, s_a)))
    m_s, alpha_s, acc_s, max_a = lax.cond(
        n_pairs > 0,
        lambda c: lax.cond(odd, lambda c2: pair(n_pairs - 1, c2),
                           lambda c2: pair(n_pairs - 1, c2, look_ahead=False), c),
        lambda c: c, carry)
    acc_s = pv(jnp.maximum(2 * n_pairs - 1, 0), p_b, alpha_s, acc_s)

    def odd_tile(args):
        m, acc = args
        m, alpha = soft(s_a, p_a, m, max_a)
        return m, pv(n_tiles - 1, p_a, alpha, acc)

    m_s, acc_s = lax.cond(odd, odd_tile, lambda args: args, (m_s, acc_s))

    d0 = pl.multiple_of(t0, qb)
    r_d = lax.broadcasted_iota(jnp.int32, (qb, qb), 0)
    c_d = lax.broadcasted_iota(jnp.int32, (qb, qb), 1)
    rows_d = _load_rows(lambda r: sel_scr[pl.ds(r, 8), :], t0 // SEL_BLOCK, qb // SEL_BLOCK)
    chosen_d = jnp.concatenate(
        [jnp.broadcast_to(rows_d[i:i + 1, :], (SEL_BLOCK, qb)) for i in range(qb // SEL_BLOCK)], axis=0)
    sb_d = (_dot_nt(ks_ref[0, 0, pl.ds(d0, qb), :], q_r)
            + jnp.concatenate([jnp.where(r_d <= c_d, chosen_d, NEG_INF)] * ng, axis=1))
    m_d = jnp.maximum(m_s, jnp.max(sb_d, axis=0, keepdims=True))
    acc_s = jnp.exp2(m_s - m_d) * acc_s + _dot(vst_ref[0, 0, :, pl.ds(d0, qb)], jnp.exp2(sb_d - m_d).astype(BF16))

    o_s = acc_s[0:HEAD_DIM] * (1.0 / acc_s[HEAD_DIM:HEAD_DIM + 1])
    o_w = acc_w[0:HEAD_DIM] * (1.0 / acc_w[HEAD_DIM:HEAD_DIM + 1])
    g_c, g_s, g_w = [_load_rows(lambda r: gt_ref[0, pl.ds(r, 8), :], br * NSA_HEADS + hk * ng, ng)
                     for br in range(N_BRANCH)]
    for g in range(ng):
        sl = slice(g * qb, (g + 1) * qb)
        mix = g_c[g:g + 1, :] * o_c[:, sl] + g_s[g:g + 1, :] * o_s[:, sl] + g_w[g:g + 1, :] * o_w[:, sl]
        o_ref[0, g * HEAD_DIM:(g + 1) * HEAD_DIM, :] = mix.astype(BF16)


def _attention(qn, qr, gt, kc, vct, ks, vst, kw, vwt, ovl_t, batch, seq):
    assert seq % (2 * SEL_BIG) == 0 and seq >= WINDOW + Q_TILE and seq // SEL_BLOCK >= N_SELECT
    n_cmp = seq // CMP_STRIDE
    n_sel = seq // SEL_BLOCK
    qb = Q_TILE
    n_gate = N_BRANCH * NSA_HEADS
    q_spec = pl.BlockSpec((1, NSA_GROUP, qb, HEAD_DIM), lambda b, hk, i: (b, hk, i, 0))
    nat_full = pl.BlockSpec((1, 1, seq, HEAD_DIM), lambda b, hk, i: (b, hk, 0, 0))
    tr_full = pl.BlockSpec((1, 1, V_ROWS, seq), lambda b, hk, i: (b, hk, 0, 0))
    return pl.pallas_call(
        functools.partial(_attn_kernel, n_cmp=n_cmp, n_sel=n_sel),
        grid=(batch, NSA_KV_HEADS, seq // qb),
        in_specs=[
            q_spec, q_spec,
            pl.BlockSpec((1, n_gate, qb), lambda b, hk, i: (b, 0, i)),
            pl.BlockSpec((1, 1, n_cmp, HEAD_DIM), lambda b, hk, i: (b, hk, 0, 0)),
            pl.BlockSpec((1, 1, V_ROWS, n_cmp), lambda b, hk, i: (b, hk, 0, 0)),
            nat_full, tr_full, nat_full, tr_full,
            _const_spec((n_sel, n_cmp)),
        ],
        out_specs=pl.BlockSpec((1, NSA_GROUP * HEAD_DIM, qb), lambda b, hk, i: (b, hk, i)),
        out_shape=jax.ShapeDtypeStruct((batch, D_MODEL, seq), BF16),
        scratch_shapes=[
            pltpu.VMEM((n_sel, qb), F32),
            pltpu.VMEM((HEAD_DIM, NSA_GROUP * qb), F32),
            pltpu.VMEM((SEL_BIG, NSA_GROUP * qb), F32),
            pltpu.VMEM((SEL_BIG, NSA_GROUP * qb), F32),
            pltpu.VMEM((SEL_BIG, NSA_GROUP * qb), BF16),
            pltpu.VMEM((SEL_BIG, NSA_GROUP * qb), BF16),
        ],
        compiler_params=_params(("arbitrary", "arbitrary", "arbitrary")),
        name="nsa_attention",
    )(qn, qr, gt, kc, vct, ks, vst, kw, vwt, ovl_t)


def _rope_lane_tables(seq):
    inv_freq = 1.0 / (ROPE_THETA ** (jnp.arange(0, HEAD_DIM, 2, dtype=F32) / HEAD_DIM))
    ang = jnp.arange(seq, dtype=F32)[:, None] * inv_freq[None, :]
    cos, sin = jnp.cos(ang), jnp.sin(ang)
    return jnp.tile(cos, (1, 4)), jnp.concatenate([-sin, sin, -sin, sin], axis=-1)


def _overlap_t(n_sel, n_cmp):
    cs = jnp.arange(n_cmp)[None, :] * CMP_STRIDE
    ss = jnp.arange(n_sel)[:, None] * SEL_BLOCK
    return ((cs < ss + SEL_BLOCK) & (cs + CMP_BLOCK > ss)).astype(BF16)


def kernel(x, c, ada_w, ada_b, norm_mix, norm_ffn, hgrn_w_in, hgrn_lower_bounds, hgrn_out_norm, hgrn_w_out,
           kv_ada_w, kv_ada_b, kv_norm, nsa_w_kv, nsa_k_norm, cmp_pos, cmp_w1, cmp_w2, nsa_w_q, nsa_q_norm,
           nsa_w_out, ffn_w_in, ffn_w_out):
    batch, seq, _ = x.shape
    w = NSA_KV_HEADS * HEAD_DIM
    x2 = x.reshape(batch * seq, D_MODEL)

    c_pad = jnp.zeros((8, D_MODEL), F32).at[:batch].set(c)
    mods = _mods(c_pad, ada_w, ada_b)[:, :batch].reshape(2 * DEPTH, batch, 3, D_MODEL)
    kv_mod = _mods(c_pad, kv_ada_w[None], kv_ada_b[None])[0, :batch].reshape(batch, 2, D_MODEL)

    cosn, sinn = _rope_lane_tables(seq)
    eye_seg = jnp.kron(jnp.eye(w // HEAD_DIM, dtype=F32), jnp.ones((HEAD_DIM, HEAD_DIM), F32)).astype(BF16)
    n_rows = seq // CMP_STRIDE
    ovl_t = _overlap_t(seq // SEL_BLOCK, n_rows)
    shared = None

    for layer in range(DEPTH):
        mix_mod = mods[2 * layer]
        gain = norm_mix[layer][None, :]
        if layer < N_A_LAYERS:
            x2 = _hgrn(x2, mix_mod, gain, hgrn_w_in[layer].astype(BF16), hgrn_lower_bounds,
                       hgrn_out_norm[layer][None, :], hgrn_w_out[layer].astype(BF16), batch, seq, layer)
        else:
            if shared is None:
                wkv = nsa_w_kv.reshape(D_MODEL, 2 * N_BRANCH, w)
                w_nat = jnp.concatenate([wkv[:, 0], wkv[:, 1], wkv[:, 2], wkv[:, 4]], axis=1).astype(BF16)
                w_vt = jnp.concatenate([wkv[:, 3], wkv[:, 5]], axis=1).T.astype(BF16)
                kcmp, vcmp, ks, kw, vst, vwt = _kv_prep(
                    x2, kv_mod, kv_norm[None, :], w_nat, w_vt, eye_seg,
                    jnp.tile(nsa_k_norm[1], NSA_KV_HEADS)[None, :], jnp.tile(nsa_k_norm[2], NSA_KV_HEADS)[None, :],
                    cosn, sinn, batch, seq)
                pos8 = jnp.zeros((2, 8, CMP_BLOCK * HEAD_DIM), F32).at[:, 0].set(
                    cmp_pos.reshape(2, CMP_BLOCK * HEAD_DIM))
                rk = kcmp.reshape(batch, NSA_KV_HEADS, n_rows, CMP_STRIDE * HEAD_DIM)
                rv = vcmp.reshape(batch, NSA_KV_HEADS, n_rows, CMP_STRIDE * HEAD_DIM)
                kc = _compress(rk, cmp_w1[0].astype(BF16), pos8[0], cmp_w2[0].astype(BF16),
                               nsa_k_norm[0][None, :], batch, n_rows, False)
                vct = _compress(rv, cmp_w1[1].astype(BF16), pos8[1], cmp_w2[1].T.astype(BF16),
                                None, batch, n_rows, True)
                shared = (kc, vct, ks, vst, kw, vwt)
            bl = layer - N_A_LAYERS
            wq = nsa_w_q[bl]
            qn, qr, gt = _q_proj(x2, mix_mod, gain, wq[:, :D_MODEL].astype(BF16), wq[:, D_MODEL:].T.astype(BF16),
                                 eye_seg, jnp.tile(nsa_q_norm[bl], NSA_GROUP)[None, :], cosn, sinn, batch, seq)
            o_t = _attention(qn, qr, gt, *shared, ovl_t, batch, seq)
        ffn_args = (mods[2 * layer + 1], norm_ffn[layer][None, :], ffn_w_in[layer].astype(BF16),
                    ffn_w_out[layer].astype(BF16))
        if layer < N_A_LAYERS:
            x2 = _ffn(x2, *ffn_args, seq)
        else:
            x2 = _out_ffn(x2, mix_mod, o_t, nsa_w_out[bl].astype(BF16), *ffn_args, batch, seq)
    return x2.reshape(batch, seq, D_MODEL)
```

```python
import functools

import jax
import jax.numpy as jnp
from jax import lax
from jax.experimental import pallas as pl
from jax.experimental.pallas import tpu as pltpu

F32 = jnp.float32
BF16 = jnp.bfloat16

D_MODEL = 1024
DEPTH = 4
N_A_LAYERS = DEPTH // 2
HGRN_HEADS = 8
HGRN_DK = 128
NSA_HEADS = 16
NSA_KV_HEADS = 4
NSA_GROUP = 4
HEAD_DIM = 64
N_BRANCH = 3
CMP_BLOCK = 32
CMP_STRIDE = 16
CMP_HIDDEN = 256
SEL_BLOCK = 64
N_SELECT = 16
WINDOW = 512
ROPE_THETA = 10000.0
FFN_HIDDEN = 2816
EPS = 1e-6
NEG_INF = -1e30
FORCE_SCORE = 1e9

LANES = 128
VMEM_LIMIT = 56 * 1024 * 1024

ROW_TILE = 512
HGRN_CHUNK = 128
Q_TILE = 256
SEL_BIG = 1024
FFN_CHUNK = 256
V_ROWS = HEAD_DIM + 16
LOG2E = 1.4426950408889634
CAUSAL_VARIANTS = 8


def _params(sem):
    return pltpu.CompilerParams(dimension_semantics=sem, vmem_limit_bytes=VMEM_LIMIT)


def _const_spec(shape):
    nd = len(shape)
    return pl.BlockSpec(shape, lambda *_: (0,) * nd, pipeline_mode=pl.Buffered(1))


def _dot(a, b):
    return jnp.dot(a, b, preferred_element_type=F32)


def _dot_nt(a, b):
    return lax.dot_general(a, b, (((1,), (1,)), ((), ())), preferred_element_type=F32)


def _dot_tn(a, b):
    return lax.dot_general(a, b, (((0,), (0,)), ((), ())), preferred_element_type=F32)


def _split(a):
    hi = a.astype(BF16)
    lo = (a - hi.astype(F32)).astype(BF16)
    return hi, lo


def _sigmoid(x):
    return 1.0 / (1.0 + jnp.exp(-x))


def _silu(x):
    return x * _sigmoid(x)


def _mod_norm(x, gain, shift, scale):
    ms = jnp.mean(x * x, axis=-1, keepdims=True)
    y = x * lax.rsqrt(ms + EPS) * gain
    return y * (1.0 + scale) + shift


def _mods_kernel(c_ref, w_ref, b_ref, o_ref):
    c = c_ref[...]
    ah, al = _split(_silu(c))
    wh, wl = _split(w_ref[0])
    o_ref[0] = _dot(ah, wh) + _dot(al, wh) + _dot(ah, wl) + b_ref[0]


def _mods(c_pad, w, b):
    n_l, _, n = w.shape
    tn = 1024
    return pl.pallas_call(
        _mods_kernel,
        grid=(n_l, n // tn),
        in_specs=[
            pl.BlockSpec((8, D_MODEL), lambda l, j: (0, 0)),
            pl.BlockSpec((1, D_MODEL, tn), lambda l, j: (l, 0, j)),
            pl.BlockSpec((1, 1, tn), lambda l, j: (l, 0, j)),
        ],
        out_specs=pl.BlockSpec((1, 8, tn), lambda l, j: (l, 0, j)),
        out_shape=jax.ShapeDtypeStruct((n_l, 8, n), F32),
        compiler_params=_params(("arbitrary", "arbitrary")),
        name="adaln_mods",
    )(c_pad, w, b.reshape(n_l, 1, n))


def _ffn_body(x, m, gain, win_ref, wout_ref):
    h = _mod_norm(x, gain, m[0:1], m[1:2]).astype(BF16)
    acc = jnp.zeros(x.shape, F32)
    for c in range(FFN_HIDDEN // FFN_CHUNK):
        lo = c * FFN_CHUNK
        a = _dot(h, win_ref[:, lo:lo + FFN_CHUNK])
        b = _dot(h, win_ref[:, FFN_HIDDEN + lo:FFN_HIDDEN + lo + FFN_CHUNK])
        g = (_silu(a) * b).astype(BF16)
        acc = acc + _dot(g, wout_ref[lo:lo + FFN_CHUNK, :])
    return x + m[2:3] * acc


def _ffn_kernel(x_ref, mod_ref, gain_ref, win_ref, wout_ref, o_ref):
    o_ref[...] = _ffn_body(x_ref[...], mod_ref[0], gain_ref[...], win_ref, wout_ref)


def _out_ffn_kernel(x_ref, mixmod_ref, ot_ref, wo_ref, mod_ref, gain_ref, win_ref, wout_ref, o_ref):
    x = x_ref[...] + mixmod_ref[0][2:3] * _dot_tn(ot_ref[0], wo_ref[...])
    o_ref[...] = _ffn_body(x, mod_ref[0], gain_ref[...], win_ref, wout_ref)


def _out_ffn(x2, mix_mod, o_t, w_o, mod, gain, w_in, w_out, batch, seq):
    per_b = seq // ROW_TILE
    row_spec = pl.BlockSpec((ROW_TILE, D_MODEL), lambda b, j: (b * per_b + j, 0))
    mod_spec = pl.BlockSpec((1, 3, D_MODEL), lambda b, j: (b, 0, 0))
    return pl.pallas_call(
        _out_ffn_kernel,
        grid=(batch, per_b),
        in_specs=[
            row_spec, mod_spec,
            pl.BlockSpec((1, D_MODEL, ROW_TILE), lambda b, j: (b, 0, j)),
            _const_spec((D_MODEL, D_MODEL)),
            mod_spec,
            _const_spec((1, D_MODEL)),
            _const_spec((D_MODEL, 2 * FFN_HIDDEN)),
            _const_spec((FFN_HIDDEN, D_MODEL)),
        ],
        out_specs=row_spec,
        out_shape=jax.ShapeDtypeStruct(x2.shape, F32),
        compiler_params=_params(("arbitrary", "arbitrary")),
        name="nsa_out_ffn",
    )(x2, mix_mod, o_t, w_o, mod, gain, w_in, w_out)


def _ffn(x2, mod, gain, w_in, w_out, seq):
    t = x2.shape[0]
    per_b = seq // ROW_TILE
    return pl.pallas_call(
        _ffn_kernel,
        grid=(t // ROW_TILE,),
        in_specs=[
            pl.BlockSpec((ROW_TILE, D_MODEL), lambda i: (i, 0)),
            pl.BlockSpec((1, 3, D_MODEL), lambda i: (i // per_b, 0, 0)),
            _const_spec((1, D_MODEL)),
            _const_spec((D_MODEL, 2 * FFN_HIDDEN)),
            _const_spec((FFN_HIDDEN, D_MODEL)),
        ],
        out_specs=pl.BlockSpec((ROW_TILE, D_MODEL), lambda i: (i, 0)),
        out_shape=jax.ShapeDtypeStruct(x2.shape, F32),
        compiler_params=_params(("arbitrary",)),
        name="ffn",
    )(x2, mod, gain, w_in, w_out)


def _hgrn_kernel(x_ref, mod_ref, gain_ref, win_ref, lbraw_ref, onorm_ref, wout_ref,
                 o_ref, proj_scr, st_scr, oall_scr, *, layer):
    tc = HGRN_CHUNK
    dk = HGRN_DK

    @pl.when(pl.program_id(1) == 0)
    def _():
        st_scr[...] = jnp.zeros(st_scr.shape, F32)

    x = x_ref[...]
    m = mod_ref[0]
    h = _mod_norm(x, gain_ref[...], m[0:1], m[1:2]).astype(BF16)
    proj_scr[...] = _dot(h, win_ref[...])

    if layer > 0:
        raw = lbraw_ref[...]
        e = jnp.exp(raw - jnp.max(raw, axis=0, keepdims=True))
        sm = e / jnp.sum(e, axis=0, keepdims=True)
        lb = jnp.sum(sm[1:layer + 1], axis=0, keepdims=True)

    row = lax.broadcasted_iota(jnp.int32, (tc, D_MODEL), 0)
    r_i = lax.broadcasted_iota(jnp.int32, (tc, tc), 0)
    c_i = lax.broadcasted_iota(jnp.int32, (tc, tc), 1)
    ones_b = jnp.ones((dk, tc), BF16)
    tri = (r_i >= c_i).astype(BF16)
    n_lvl = tc.bit_length() - 1

    def chunk(ci, carry):
        r0 = pl.multiple_of(ci * tc, tc)
        qp = proj_scr[pl.ds(r0, tc), 0:D_MODEL]
        fp = proj_scr[pl.ds(r0, tc), D_MODEL:2 * D_MODEL]
        v = proj_scr[pl.ds(r0, tc), 2 * D_MODEL:3 * D_MODEL].astype(BF16)
        gp = proj_scr[pl.ds(r0, tc), 3 * D_MODEL:4 * D_MODEL]

        q = _silu(qp)
        e = jnp.exp(-jnp.abs(fp))
        r = 1.0 / (1.0 + e)
        pos = fp >= 0.0
        sig = jnp.where(pos, r, e * r)
        nsig = jnp.where(pos, e * r, r)
        if layer == 0:
            logf = jnp.minimum(fp, 0.0) - jnp.log(1.0 + e)
            kk = nsig
        else:
            logf = jnp.log(lb + (1.0 - lb) * sig)
            kk = (1.0 - lb) * nsig

        g_inc = logf * LOG2E
        g_hi = g_inc.astype(BF16)
        g_res = g_inc - g_hi.astype(F32)
        g_mid = g_res.astype(BF16)
        g_lo = (g_res - g_mid.astype(F32)).astype(BF16)
        g_cum = _dot(tri, g_hi) + _dot(tri, g_mid) + _dot(tri, g_lo)

        p_acc = [jnp.zeros((tc, tc), F32) for _ in range(HGRN_HEADS)]
        end_val = g_cum
        for lvl in range(n_lvl):
            hs = 1 << lvl
            n_b = tc // hs
            if hs < 8:
                second = (row & hs) != 0
                ref_val = jnp.where(second, pltpu.roll(end_val, hs, 0), end_val)
                nxt_val = jnp.where(second, end_val, pltpu.roll(end_val, tc - hs, 0))
            else:
                blocks = [end_val[i * hs:(i + 1) * hs] for i in range(n_b)]
                ref_val = jnp.concatenate([blocks[i - i % 2] for i in range(n_b)], axis=0)
                nxt_val = jnp.concatenate([blocks[i - i % 2 + 1] for i in range(n_b)], axis=0)
            dist = lax.bitcast_convert_type(g_cum - ref_val, jnp.uint32) | jnp.uint32(0x80000000)
            decay = jnp.exp2(lax.bitcast_convert_type(dist, F32))
            qs = (q * decay).astype(BF16)
            ks = (kk * decay).astype(BF16)
            upper = (((r_i >> lvl) - (c_i >> lvl)) == 1) & (((c_i >> lvl) & 1) == 0)
            for hd in range(HGRN_HEADS):
                sl = slice(hd * dk, (hd + 1) * dk)
                p_acc[hd] = p_acc[hd] + jnp.where(upper, _dot_nt(qs[:, sl], ks[:, sl]), 0.0)
            end_val = nxt_val

        qk = (q * kk).astype(BF16)
        g_last = g_cum[tc - 1:tc, :]
        qe = (q * jnp.exp2(g_cum)).astype(BF16)
        kd = (kk * jnp.exp2(g_last - g_cum)).astype(BF16)
        s_decay = jnp.exp2(g_last)
        gate_act = _silu(gp)
        onorm = onorm_ref[...]
        for hd in range(HGRN_HEADS):
            sl = slice(hd * dk, (hd + 1) * dk)
            p_h = p_acc[hd] + jnp.where(r_i == c_i, _dot(qk[:, sl], ones_b), 0.0)
            st = st_scr[hd]
            o_h = _dot(p_h.astype(BF16), v[:, sl]) + _dot_nt(qe[:, sl], st.astype(BF16))
            st_scr[hd] = s_decay[:, sl] * st + _dot_tn(v[:, sl], kd[:, sl])
            ms = jnp.mean(o_h * o_h, axis=-1, keepdims=True)
            o_n = o_h * lax.rsqrt(ms + EPS) * onorm
            oall_scr[pl.ds(r0, tc), sl] = (o_n * gate_act[:, sl]).astype(BF16)
        return carry

    lax.fori_loop(0, ROW_TILE // tc, chunk, 0)
    o_ref[...] = x + m[2:3] * _dot(oall_scr[...], wout_ref[...])


def _hgrn(x2, mod, gain, w_in, lb_raw, onorm, w_out, batch, seq, layer):
    per_b = seq // ROW_TILE
    return pl.pallas_call(
        functools.partial(_hgrn_kernel, layer=layer),
        grid=(batch, per_b),
        in_specs=[
            pl.BlockSpec((ROW_TILE, D_MODEL), lambda b, j: (b * per_b + j, 0)),
            pl.BlockSpec((1, 3, D_MODEL), lambda b, j: (b, 0, 0)),
            _const_spec((1, D_MODEL)),
            _const_spec((D_MODEL, 4 * D_MODEL)),
            _const_spec((N_A_LAYERS, D_MODEL)),
            _const_spec((1, HGRN_DK)),
            _const_spec((D_MODEL, D_MODEL)),
        ],
        out_specs=pl.BlockSpec((ROW_TILE, D_MODEL), lambda b, j: (b * per_b + j, 0)),
        out_shape=jax.ShapeDtypeStruct(x2.shape, F32),
        scratch_shapes=[
            pltpu.VMEM((ROW_TILE, 4 * D_MODEL), F32),
            pltpu.VMEM((HGRN_HEADS, HGRN_DK, HGRN_DK), F32),
            pltpu.VMEM((ROW_TILE, D_MODEL), BF16),
        ],
        compiler_params=_params(("arbitrary", "arbitrary")),
        name=f"hgrn{layer}",
    )(x2, mod, gain, w_in, lb_raw, onorm, w_out)


def _segnorm64(xc, seg_ones, gain):
    hi, lo = _split(xc * xc)
    ss = _dot(hi, seg_ones) + _dot(lo, seg_ones)
    return xc * lax.rsqrt(ss * (1.0 / HEAD_DIM) + EPS) * gain


def _load_rows(load8, start, n):
    blk = load8(pl.multiple_of(start // 8 * 8, 8))
    off = start % 8
    out = blk[0:n]
    for o in range(1, 8 // n):
        out = jnp.where(off == o * n, blk[o * n:(o + 1) * n], out)
    return out


def _rope_lanes(xp, cosn, sinn):
    lane = lax.broadcasted_iota(jnp.int32, xp.shape, 1)
    first = (lane & (HEAD_DIM // 2)) == 0
    rot = jnp.where(first, pltpu.roll(xp, LANES - HEAD_DIM // 2, 1), pltpu.roll(xp, HEAD_DIM // 2, 1))
    return xp * cosn + rot * sinn


def _rope_wide(xc, cosn, sinn):
    return jnp.concatenate(
        [_rope_lanes(xc[:, i * LANES:(i + 1) * LANES], cosn, sinn) for i in range(xc.shape[1] // LANES)], axis=1)


def _kv_kernel(x_ref, mod_ref, gain_ref, wn_ref, wvt_ref, seg_ref, gsel_ref, gwin_ref, cos_ref, sin_ref,
               kcmp_ref, vcmp_ref, ksel_ref, kwin_ref, vselt_ref, vwint_ref):
    x = x_ref[...]
    m = mod_ref[0]
    h = _mod_norm(x, gain_ref[...], m[0:1], m[1:2]).astype(BF16)
    nat = _dot(h, wn_ref[...])
    vt = _dot_nt(wvt_ref[...], h)
    w = NSA_KV_HEADS * HEAD_DIM
    seg = seg_ref[...]
    cosn = cos_ref[...]
    sinn = sin_ref[...]
    kcmp = nat[:, 0:w].astype(BF16)
    vcmp = nat[:, w:2 * w].astype(BF16)
    ksel = _rope_wide(_segnorm64(nat[:, 2 * w:3 * w], seg, gsel_ref[...]), cosn, sinn).astype(BF16)
    kwin = _rope_wide(_segnorm64(nat[:, 3 * w:4 * w], seg, gwin_ref[...]), cosn, sinn).astype(BF16)
    ones = jnp.ones((V_ROWS - HEAD_DIM, x.shape[0]), BF16)
    for hd in range(NSA_KV_HEADS):
        sl = slice(hd * HEAD_DIM, (hd + 1) * HEAD_DIM)
        kcmp_ref[0, hd] = kcmp[:, sl]
        vcmp_ref[0, hd] = vcmp[:, sl]
        ksel_ref[0, hd] = ksel[:, sl]
        kwin_ref[0, hd] = kwin[:, sl]
        vselt_ref[0, hd, 0:HEAD_DIM, :] = vt[hd * HEAD_DIM:(hd + 1) * HEAD_DIM, :].astype(BF16)
        vselt_ref[0, hd, HEAD_DIM:V_ROWS, :] = ones
        vwint_ref[0, hd, 0:HEAD_DIM, :] = vt[w + hd * HEAD_DIM:w + (hd + 1) * HEAD_DIM, :].astype(BF16)
        vwint_ref[0, hd, HEAD_DIM:V_ROWS, :] = ones


def _kv_prep(x2, mod, gain, w_nat, w_vt, seg, gsel, gwin, cosn, sinn, batch, seq):
    per_b = seq // ROW_TILE
    w = NSA_KV_HEADS * HEAD_DIM
    nat_spec = pl.BlockSpec((1, NSA_KV_HEADS, ROW_TILE, HEAD_DIM), lambda b, j: (b, 0, j, 0))
    tr_spec = pl.BlockSpec((1, NSA_KV_HEADS, V_ROWS, ROW_TILE), lambda b, j: (b, 0, 0, j))
    nat_shape = jax.ShapeDtypeStruct((batch, NSA_KV_HEADS, seq, HEAD_DIM), BF16)
    tr_shape = jax.ShapeDtypeStruct((batch, NSA_KV_HEADS, V_ROWS, seq), BF16)
    return pl.pallas_call(
        _kv_kernel,
        grid=(batch, per_b),
        in_specs=[
            pl.BlockSpec((ROW_TILE, D_MODEL), lambda b, j: (b * per_b + j, 0)),
            pl.BlockSpec((1, 2, D_MODEL), lambda b, j: (b, 0, 0)),
            _const_spec((1, D_MODEL)),
            _const_spec((D_MODEL, 4 * w)),
            _const_spec((2 * w, D_MODEL)),
            _const_spec((w, w)),
            _const_spec((1, w)),
            _const_spec((1, w)),
            pl.BlockSpec((ROW_TILE, LANES), lambda b, j: (j, 0)),
            pl.BlockSpec((ROW_TILE, LANES), lambda b, j: (j, 0)),
        ],
        out_specs=[nat_spec, nat_spec, nat_spec, nat_spec, tr_spec, tr_spec],
        out_shape=[nat_shape, nat_shape, nat_shape, nat_shape, tr_shape, tr_shape],
        compiler_params=_params(("arbitrary", "arbitrary")),
        name="nsa_kv",
    )(x2, mod, gain, w_nat, w_vt, seg, gsel, gwin, cosn, sinn)


def _compress_pre(r_ref, w1_ref, pos_ref):
    r = r_ref[0, 0]
    half = CMP_STRIDE * HEAD_DIM
    w1a = w1_ref[0:half, :]
    w1b = w1_ref[half:2 * half, :]
    n_rows = r.shape[0]
    u = _dot(r, w1a)
    v = _dot(r, w1b)
    pos = pos_ref[...]
    ph, pl_ = _split(pos)
    bias = (_dot(ph[:, 0:half], w1a) + _dot(pl_[:, 0:half], w1a)
            + _dot(ph[:, half:], w1b) + _dot(pl_[:, half:], w1b))[0:1, :]
    pre = u + pltpu.roll(v, n_rows - 1, 0) + bias
    return _silu(pre).astype(BF16)


def _compress_k_kernel(r_ref, w1_ref, pos_ref, w2_ref, gain_ref, o_ref):
    hid = _compress_pre(r_ref, w1_ref, pos_ref)
    out = _dot(hid, w2_ref[...])
    ms = jnp.mean(out * out, axis=-1, keepdims=True)
    o_ref[0, 0] = (out * lax.rsqrt(ms + EPS) * gain_ref[...]).astype(BF16)


def _compress_v_kernel(r_ref, w1_ref, pos_ref, w2t_ref, o_ref):
    hid = _compress_pre(r_ref, w1_ref, pos_ref)
    o_ref[0, 0, 0:HEAD_DIM, :] = _dot_nt(w2t_ref[...], hid).astype(BF16)
    o_ref[0, 0, HEAD_DIM:V_ROWS, :] = jnp.ones((V_ROWS - HEAD_DIM, hid.shape[0]), BF16)


def _compress(r, w1, pos8, w2, gain, batch, n_rows, transposed):
    half2 = CMP_BLOCK * HEAD_DIM
    in_specs = [
        pl.BlockSpec((1, 1, n_rows, CMP_STRIDE * HEAD_DIM), lambda b, hd: (b, hd, 0, 0)),
        _const_spec((half2, CMP_HIDDEN)),
        _const_spec((8, half2)),
    ]
    if transposed:
        kern = _compress_v_kernel
        in_specs.append(_const_spec((HEAD_DIM, CMP_HIDDEN)))
        args = (r, w1, pos8, w2)
        out_spec = pl.BlockSpec((1, 1, V_ROWS, n_rows), lambda b, hd: (b, hd, 0, 0))
        out_shape = jax.ShapeDtypeStruct((batch, NSA_KV_HEADS, V_ROWS, n_rows), BF16)
    else:
        kern = _compress_k_kernel
        in_specs += [_const_spec((CMP_HIDDEN, HEAD_DIM)), _const_spec((1, HEAD_DIM))]
        args = (r, w1, pos8, w2, gain)
        out_spec = pl.BlockSpec((1, 1, n_rows, HEAD_DIM), lambda b, hd: (b, hd, 0, 0))
        out_shape = jax.ShapeDtypeStruct((batch, NSA_KV_HEADS, n_rows, HEAD_DIM), BF16)
    return pl.pallas_call(
        kern,
        grid=(batch, NSA_KV_HEADS),
        in_specs=in_specs,
        out_specs=out_spec,
        out_shape=out_shape,
        compiler_params=_params(("arbitrary", "arbitrary")),
        name="nsa_compress_v" if transposed else "nsa_compress_k",
    )(*args)


def _q_kernel(x_ref, mod_ref, gain_ref, wq_ref, wgt_ref, seg_ref, qgain_ref, cos_ref, sin_ref,
              qn_ref, qr_ref, gt_ref):
    x = x_ref[...]
    m = mod_ref[0]
    h = _mod_norm(x, gain_ref[...], m[0:1], m[1:2]).astype(BF16)
    gt_ref[0] = _sigmoid(_dot_nt(wgt_ref[...], h))
    seg = seg_ref[...]
    cosn = cos_ref[...]
    sinn = sin_ref[...]
    scale = HEAD_DIM ** -0.5 * LOG2E
    w = NSA_GROUP * HEAD_DIM
    for c in range(NSA_KV_HEADS):
        qc = _dot(h, wq_ref[:, c * w:(c + 1) * w])
        qn = _segnorm64(qc, seg, qgain_ref[...]) * scale
        qr = _rope_wide(qn, cosn, sinn)
        qn = qn.astype(BF16)
        qr = qr.astype(BF16)
        for g in range(NSA_GROUP):
            sl = slice(g * HEAD_DIM, (g + 1) * HEAD_DIM)
            qn_ref[0, c * NSA_GROUP + g] = qn[:, sl]
            qr_ref[0, c * NSA_GROUP + g] = qr[:, sl]


def _q_proj(x2, mod, gain, w_q, w_gt, seg, qgain, cosn, sinn, batch, seq):
    per_b = seq // ROW_TILE
    w = NSA_GROUP * HEAD_DIM
    n_gate = N_BRANCH * NSA_HEADS
    q_spec = pl.BlockSpec((1, NSA_HEADS, ROW_TILE, HEAD_DIM), lambda b, j: (b, 0, j, 0))
    q_shape = jax.ShapeDtypeStruct((batch, NSA_HEADS, seq, HEAD_DIM), BF16)
    return pl.pallas_call(
        _q_kernel,
        grid=(batch, per_b),
        in_specs=[
            pl.BlockSpec((ROW_TILE, D_MODEL), lambda b, j: (b * per_b + j, 0)),
            pl.BlockSpec((1, 3, D_MODEL), lambda b, j: (b, 0, 0)),
            _const_spec((1, D_MODEL)),
            _const_spec((D_MODEL, D_MODEL)),
            _const_spec((n_gate, D_MODEL)),
            _const_spec((w, w)),
            _const_spec((1, w)),
            pl.BlockSpec((ROW_TILE, LANES), lambda b, j: (j, 0)),
            pl.BlockSpec((ROW_TILE, LANES), lambda b, j: (j, 0)),
        ],
        out_specs=[q_spec, q_spec, pl.BlockSpec((1, n_gate, ROW_TILE), lambda b, j: (b, 0, j))],
        out_shape=[q_shape, q_shape, jax.ShapeDtypeStruct((batch, n_gate, seq), F32)],
        compiler_params=_params(("arbitrary", "arbitrary")),
        name="nsa_q",
    )(x2, mod, gain, w_q, w_gt, seg, qgain, cosn, sinn)


def _attn_kernel(qn_ref, qr_ref, gt_ref, kc_ref, vct_ref, ks_ref, vst_ref, kw_ref, vwt_ref, ovl_ref,
                 o_ref, sel_scr, oc_scr, s_a, s_b, p_a, p_b, *, n_cmp, n_sel):
    qb = Q_TILE
    ng = NSA_GROUP
    hk = pl.program_id(1)
    qi = pl.program_id(2)
    t0 = qi * qb
    q_n = qn_ref[0].reshape(ng * qb, HEAD_DIM)
    q_r = qr_ref[0].reshape(ng * qb, HEAD_DIM)

    def tq(rows, cols):
        lane = lax.broadcasted_iota(jnp.int32, (rows, cols), 1)
        return t0 + (lane & (qb - 1))

    n_win = WINDOW + qb
    w0 = pl.multiple_of(jnp.maximum(t0 - WINDOW, 0), qb)
    s_w = _dot_nt(kw_ref[0, 0, pl.ds(w0, n_win), :], q_r)
    kpos = w0 + lax.broadcasted_iota(jnp.int32, (n_win, qb), 0)
    t_w = tq(n_win, qb)
    bias_w = jnp.where((kpos <= t_w) & (kpos > t_w - WINDOW), 0.0, NEG_INF)
    sb_w = s_w + jnp.concatenate([bias_w] * ng, axis=1)
    p_w = jnp.exp2(sb_w - jnp.max(sb_w, axis=0, keepdims=True)).astype(BF16)
    acc_w = _dot(vwt_ref[0, 0, :, pl.ds(w0, n_win)], p_w)

    def compress_and_select(n_c, n_s):
        c_end = lax.broadcasted_iota(jnp.int32, (n_c, qb), 0) * CMP_STRIDE + (CMP_BLOCK - 1)
        bias_c = jnp.where(c_end <= tq(n_c, qb), 0.0, NEG_INF)
        sb_c = _dot_nt(kc_ref[0, 0, 0:n_c, :], q_n) + jnp.concatenate([bias_c] * ng, axis=1)
        m_c = jnp.max(sb_c, axis=0, keepdims=True)
        e_c = jnp.exp2(sb_c - m_c).astype(BF16)
        acc_c = _dot(vct_ref[0, 0, :, 0:n_c], e_c)
        inv_c = jnp.where(m_c > 0.5 * NEG_INF, 1.0 / acc_c[HEAD_DIM:HEAD_DIM + 1], 0.0)
        oc_scr[...] = acc_c[0:HEAD_DIM] * inv_c

        imp_g = _dot(ovl_ref[0:n_s, 0:n_c], e_c) * inv_c
        imp = imp_g[:, 0:qb]
        for g in range(1, ng):
            imp = imp + imp_g[:, g * qb:(g + 1) * qb]
        j_i = lax.broadcasted_iota(jnp.int32, (n_s, qb), 0)
        cur = (t0 + lax.broadcasted_iota(jnp.int32, (n_s, qb), 1)) // SEL_BLOCK
        valid = j_i <= cur
        forced = (j_i == 0) | (j_i == cur) | (j_i == cur - 1)
        val = jnp.where(valid & jnp.logical_not(forced), imp, -1.0)
        for _ in range(N_SELECT - 3):
            mx = jnp.max(val, axis=0, keepdims=True)
            idx = jnp.min(jnp.where(val == mx, j_i, n_s), axis=0, keepdims=True)
            idx = jnp.where(mx >= 0.0, idx, -1)
            val = jnp.where(j_i == idx, -3.0e38, val)
        chosen = (forced & valid) | (val < -1.0e38)
        sel_scr[0:n_s, :] = jnp.where(chosen, 0.0, NEG_INF)
        if n_s < n_sel:
            sel_scr[n_s:n_sel, :] = jnp.full((n_sel - n_s, qb), NEG_INF, F32)

    n_var = min(CAUSAL_VARIANTS, n_cmp // LANES)
    variant = (t0 + qb - 1) // (n_sel * SEL_BLOCK // n_var)
    for v in range(n_var):
        pl.when(variant == v)(functools.partial(compress_and_select, (v + 1) * n_cmp // n_var, (v + 1) * n_sel // n_var))
    o_c = oc_scr[...]

    n_blk = SEL_BIG // SEL_BLOCK

    def qk(kt, s_ref):
        k0 = pl.multiple_of(kt * SEL_BIG, SEL_BIG)
        blk = kt * n_blk + lax.broadcasted_iota(jnp.int32, (n_blk, qb), 0)
        rows = jnp.where(blk < t0 // SEL_BLOCK, sel_scr[pl.ds(kt * n_blk, n_blk), :], NEG_INF)
        bias = jnp.concatenate(
            [jnp.broadcast_to(rows[i:i + 1, :], (SEL_BLOCK, qb)) for i in range(n_blk)], axis=0)
        sb = _dot_nt(ks_ref[0, 0, pl.ds(k0, SEL_BIG), :], q_r) + jnp.concatenate([bias] * ng, axis=1)
        s_ref[...] = sb
        return jnp.max(sb, axis=0, keepdims=True)

    def soft(s_ref, p_ref, m_old, tile_max):
        m_new = jnp.maximum(m_old, tile_max)
        p_ref[...] = jnp.exp2(s_ref[...] - m_new).astype(BF16)
        return m_new, jnp.exp2(m_old - m_new)

    def pv(kt, p_ref, alpha, acc):
        k0 = pl.multiple_of(kt * SEL_BIG, SEL_BIG)
        return alpha * acc + _dot(vst_ref[0, 0, :, pl.ds(k0, SEL_BIG)], p_ref[...])

    def pair(j, carry, look_ahead=True):
        m, alpha_prev, acc, max_a = carry
        ta = 2 * j
        max_b = qk(ta + 1, s_b)
        m, alpha_a = soft(s_a, p_a, m, max_a)
        acc = pv(jnp.maximum(ta - 1, 0), p_b, alpha_prev, acc)
        if look_ahead:
            max_a = qk(ta + 2, s_a)
        m, alpha_b = soft(s_b, p_b, m, max_b)
        acc = pv(ta, p_a, alpha_a, acc)
        return m, alpha_b, acc, max_a

    n_tiles = (t0 + SEL_BIG - 1) // SEL_BIG
    n_pairs = n_tiles // 2
    odd = n_tiles % 2 == 1
    p_b[...] = jnp.zeros(p_b.shape, BF16)
    carry = lax.fori_loop(
        0, jnp.maximum(n_pairs - 1, 0), pair,
        (jnp.full((1, ng * qb), NEG_INF, F32), jnp.ones((1, ng * qb), F32), jnp.zeros((V_ROWS, ng * qb), F32),
         qk(0, s_a)))
    m_s, alpha_s, acc_s, max_a = lax.cond(
        n_pairs > 0,
        lambda c: lax.cond(odd, lambda c2: pair(n_pairs - 1, c2),
                           lambda c2: pair(n_pairs - 1, c2, look_ahead=False), c),
        lambda c: c, carry)
    acc_s = pv(jnp.maximum(2 * n_pairs - 1, 0), p_b, alpha_s, acc_s)

    def odd_tile(args):
        m, acc = args
        m, alpha = soft(s_a, p_a, m, max_a)
        return m, pv(n_tiles - 1, p_a, alpha, acc)

    m_s, acc_s = lax.cond(odd, odd_tile, lambda args: args, (m_s, acc_s))

    d0 = pl.multiple_of(t0, qb)
    r_d = lax.broadcasted_iota(jnp.int32, (qb, qb), 0)
    c_d = lax.broadcasted_iota(jnp.int32, (qb, qb), 1)
    rows_d = _load_rows(lambda r: sel_scr[pl.ds(r, 8), :], t0 // SEL_BLOCK, qb // SEL_BLOCK)
    chosen_d = jnp.concatenate(
        [jnp.broadcast_to(rows_d[i:i + 1, :], (SEL_BLOCK, qb)) for i in range(qb // SEL_BLOCK)], axis=0)
    sb_d = (_dot_nt(ks_ref[0, 0, pl.ds(d0, qb), :], q_r)
            + jnp.concatenate([jnp.where(r_d <= c_d, chosen_d, NEG_INF)] * ng, axis=1))
    m_d = jnp.maximum(m_s, jnp.max(sb_d, axis=0, keepdims=True))
    acc_s = jnp.exp2(m_s - m_d) * acc_s + _dot(vst_ref[0, 0, :, pl.ds(d0, qb)], jnp.exp2(sb_d - m_d).astype(BF16))

    o_s = acc_s[0:HEAD_DIM] * (1.0 / acc_s[HEAD_DIM:HEAD_DIM + 1])
    o_w = acc_w[0:HEAD_DIM] * (1.0 / acc_w[HEAD_DIM:HEAD_DIM + 1])
    g_c, g_s, g_w = [_load_rows(lambda r: gt_ref[0, pl.ds(r, 8), :], br * NSA_HEADS + hk * ng, ng)
                     for br in range(N_BRANCH)]
    for g in range(ng):
        sl = slice(g * qb, (g + 1) * qb)
        mix = g_c[g:g + 1, :] * o_c[:, sl] + g_s[g:g + 1, :] * o_s[:, sl] + g_w[g:g + 1, :] * o_w[:, sl]
        o_ref[0, g * HEAD_DIM:(g + 1) * HEAD_DIM, :] = mix.astype(BF16)


def _attention(qn, qr, gt, kc, vct, ks, vst, kw, vwt, ovl_t, batch, seq):
    assert seq % (2 * SEL_BIG) == 0 and seq >= WINDOW + Q_TILE and seq // SEL_BLOCK >= N_SELECT
    n_cmp = seq // CMP_STRIDE
    n_sel = seq // SEL_BLOCK
    qb = Q_TILE
    n_gate = N_BRANCH * NSA_HEADS
    q_spec = pl.BlockSpec((1, NSA_GROUP, qb, HEAD_DIM), lambda b, hk, i: (b, hk, i, 0))
    nat_full = pl.BlockSpec((1, 1, seq, HEAD_DIM), lambda b, hk, i: (b, hk, 0, 0))
    tr_full = pl.BlockSpec((1, 1, V_ROWS, seq), lambda b, hk, i: (b, hk, 0, 0))
    return pl.pallas_call(
        functools.partial(_attn_kernel, n_cmp=n_cmp, n_sel=n_sel),
        grid=(batch, NSA_KV_HEADS, seq // qb),
        in_specs=[
            q_spec, q_spec,
            pl.BlockSpec((1, n_gate, qb), lambda b, hk, i: (b, 0, i)),
            pl.BlockSpec((1, 1, n_cmp, HEAD_DIM), lambda b, hk, i: (b, hk, 0, 0)),
            pl.BlockSpec((1, 1, V_ROWS, n_cmp), lambda b, hk, i: (b, hk, 0, 0)),
            nat_full, tr_full, nat_full, tr_full,
            _const_spec((n_sel, n_cmp)),
        ],
        out_specs=pl.BlockSpec((1, NSA_GROUP * HEAD_DIM, qb), lambda b, hk, i: (b, hk, i)),
        out_shape=jax.ShapeDtypeStruct((batch, D_MODEL, seq), BF16),
        scratch_shapes=[
            pltpu.VMEM((n_sel, qb), F32),
            pltpu.VMEM((HEAD_DIM, NSA_GROUP * qb), F32),
            pltpu.VMEM((SEL_BIG, NSA_GROUP * qb), F32),
            pltpu.VMEM((SEL_BIG, NSA_GROUP * qb), F32),
            pltpu.VMEM((SEL_BIG, NSA_GROUP * qb), BF16),
            pltpu.VMEM((SEL_BIG, NSA_GROUP * qb), BF16),
        ],
        compiler_params=_params(("arbitrary", "arbitrary", "arbitrary")),
        name="nsa_attention",
    )(qn, qr, gt, kc, vct, ks, vst, kw, vwt, ovl_t)


def _rope_lane_tables(seq):
    inv_freq = 1.0 / (ROPE_THETA ** (jnp.arange(0, HEAD_DIM, 2, dtype=F32) / HEAD_DIM))
    ang = jnp.arange(seq, dtype=F32)[:, None] * inv_freq[None, :]
    cos, sin = jnp.cos(ang), jnp.sin(ang)
    return jnp.tile(cos, (1, 4)), jnp.concatenate([-sin, sin, -sin, sin], axis=-1)


def _overlap_t(n_sel, n_cmp):
    cs = jnp.arange(n_cmp)[None, :] * CMP_STRIDE
    ss = jnp.arange(n_sel)[:, None] * SEL_BLOCK
    return ((cs < ss + SEL_BLOCK) & (cs + CMP_BLOCK > ss)).astype(BF16)


def kernel(x, c, ada_w, ada_b, norm_mix, norm_ffn, hgrn_w_in, hgrn_lower_bounds, hgrn_out_norm, hgrn_w_out,
           kv_ada_w, kv_ada_b, kv_norm, nsa_w_kv, nsa_k_norm, cmp_pos, cmp_w1, cmp_w2, nsa_w_q, nsa_q_norm,
           nsa_w_out, ffn_w_in, ffn_w_out):
    batch, seq, _ = x.shape
    w = NSA_KV_HEADS * HEAD_DIM
    x2 = x.reshape(batch * seq, D_MODEL)

    c_pad = jnp.zeros((8, D_MODEL), F32).at[:batch].set(c)
    mods = _mods(c_pad, ada_w, ada_b)[:, :batch].reshape(2 * DEPTH, batch, 3, D_MODEL)
    kv_mod = _mods(c_pad, kv_ada_w[None], kv_ada_b[None])[0, :batch].reshape(batch, 2, D_MODEL)

    cosn, sinn = _rope_lane_tables(seq)
    eye_seg = jnp.kron(jnp.eye(w // HEAD_DIM, dtype=F32), jnp.ones((HEAD_DIM, HEAD_DIM), F32)).astype(BF16)
    n_rows = seq // CMP_STRIDE
    ovl_t = _overlap_t(seq // SEL_BLOCK, n_rows)
    shared = None

    for layer in range(DEPTH):
        mix_mod = mods[2 * layer]
        gain = norm_mix[layer][None, :]
        if layer < N_A_LAYERS:
            x2 = _hgrn(x2, mix_mod, gain, hgrn_w_in[layer].astype(BF16), hgrn_lower_bounds,
                       hgrn_out_norm[layer][None, :], hgrn_w_out[layer].astype(BF16), batch, seq, layer)
        else:
            if shared is None:
                wkv = nsa_w_kv.reshape(D_MODEL, 2 * N_BRANCH, w)
                w_nat = jnp.concatenate([wkv[:, 0], wkv[:, 1], wkv[:, 2], wkv[:, 4]], axis=1).astype(BF16)
                w_vt = jnp.concatenate([wkv[:, 3], wkv[:, 5]], axis=1).T.astype(BF16)
                kcmp, vcmp, ks, kw, vst, vwt = _kv_prep(
                    x2, kv_mod, kv_norm[None, :], w_nat, w_vt, eye_seg,
                    jnp.tile(nsa_k_norm[1], NSA_KV_HEADS)[None, :], jnp.tile(nsa_k_norm[2], NSA_KV_HEADS)[None, :],
                    cosn, sinn, batch, seq)
                pos8 = jnp.zeros((2, 8, CMP_BLOCK * HEAD_DIM), F32).at[:, 0].set(
                    cmp_pos.reshape(2, CMP_BLOCK * HEAD_DIM))
                rk = kcmp.reshape(batch, NSA_KV_HEADS, n_rows, CMP_STRIDE * HEAD_DIM)
                rv = vcmp.reshape(batch, NSA_KV_HEADS, n_rows, CMP_STRIDE * HEAD_DIM)
                kc = _compress(rk, cmp_w1[0].astype(BF16), pos8[0], cmp_w2[0].astype(BF16),
                               nsa_k_norm[0][None, :], batch, n_rows, False)
                vct = _compress(rv, cmp_w1[1].astype(BF16), pos8[1], cmp_w2[1].T.astype(BF16),
                                None, batch, n_rows, True)
                shared = (kc, vct, ks, vst, kw, vwt)
            bl = layer - N_A_LAYERS
            wq = nsa_w_q[bl]
            qn, qr, gt = _q_proj(x2, mix_mod, gain, wq[:, :D_MODEL].astype(BF16), wq[:, D_MODEL:].T.astype(BF16),
                                 eye_seg, jnp.tile(nsa_q_norm[bl], NSA_GROUP)[None, :], cosn, sinn, batch, seq)
            o_t = _attention(qn, qr, gt, *shared, ovl_t, batch, seq)
        ffn_args = (mods[2 * layer + 1], norm_ffn[layer][None, :], ffn_w_in[layer].astype(BF16),
                    ffn_w_out[layer].astype(BF16))
        if layer < N_A_LAYERS:
            x2 = _ffn(x2, *ffn_args, seq)
        else:
            x2 = _out_ffn(x2, mix_mod, o_t, nsa_w_out[bl].astype(BF16), *ffn_args, batch, seq)
    return x2.reshape(batch, seq, D_MODEL)
```
